```python
import math
import jax, jax.numpy as jnp
from jax import lax
import numpy as np

D_MODEL = 1024
BATCH = 32
SEQ = 2048
DEPTH = 1

HEAD_DIM = 64
DIFF_V_DIM = 2 * HEAD_DIM
DIFF_HEADS = D_MODEL // (2 * DIFF_V_DIM)
SB_HEAD_DIM = HEAD_DIM
SB_HEADS = D_MODEL // (2 * SB_HEAD_DIM)
MIX_WIDTH = DIFF_HEADS * DIFF_V_DIM + SB_HEADS * SB_HEAD_DIM
DIFF_QK = DIFF_HEADS * 2 * HEAD_DIM
DIFF_V = DIFF_HEADS * DIFF_V_DIM
SB_QKV = SB_HEADS * SB_HEAD_DIM
IN_COLS = 2 * DIFF_QK + DIFF_V + 3 * SB_QKV
ROPE_DIM = HEAD_DIM // 4
ROPE_THETA = 500000.0
D_FF = ((8 * D_MODEL // 3 + 255) // 256) * 256
Q_BLOCK = 128
NORM_EPS = 1e-5

kernel_name = "hybrid_diffattn_stickbreaking_swiglu"


def rmsnorm(x, g, eps=NORM_EPS):
    xf = x.astype(jnp.float32)
    y = xf * lax.rsqrt(jnp.mean(xf * xf, axis=-1, keepdims=True) + eps)
    return (y * g.astype(jnp.float32)).astype(x.dtype)


def partial_rope(t, positions):
    half = ROPE_DIM // 2
    inv_freq = ROPE_THETA ** (-jnp.arange(half, dtype=jnp.float32) / half)
    ang = positions.astype(jnp.float32)[..., None] * inv_freq
    cos = jnp.cos(ang)[:, :, None, :]
    sin = jnp.sin(ang)[:, :, None, :]
    tf = t.astype(jnp.float32)
    x1, x2, rest = tf[..., :half], tf[..., half:ROPE_DIM], tf[..., ROPE_DIM:]
    return jnp.concatenate([x1 * cos - x2 * sin, x2 * cos + x1 * sin, rest], axis=-1)


def to_blocks(t):
    b, s = t.shape[:2]
    return t.reshape(b, s // Q_BLOCK, Q_BLOCK, *t.shape[2:]).swapaxes(0, 1)


def from_blocks(t):
    nb, b, qb = t.shape[:3]
    return t.swapaxes(0, 1).reshape(b, nb * qb, *t.shape[3:])


def differential_attention(q, k, v, lam):
    b, s = q.shape[:2]
    scale = HEAD_DIM ** -0.5
    key_idx = jnp.arange(s)

    def block(args):
        qb, i = args
        sc = jnp.einsum('bqhd,bkhd->bhqk', qb, k) * scale
        q_idx = i * Q_BLOCK + jnp.arange(Q_BLOCK)
        causal = key_idx[None, :] <= q_idx[:, None]
        sc = jnp.where(causal, sc, -jnp.inf)
        p = jax.nn.softmax(sc, axis=-1).reshape(b, DIFF_HEADS, 2, Q_BLOCK, s)
        a = p[:, :, 0] - lam * p[:, :, 1]
        return jnp.einsum('bhqk,bkhd->bqhd', a, v)

    out = lax.map(block, (to_blocks(q), jnp.arange(s // Q_BLOCK)))
    return from_blocks(out)


def stick_breaking_attention(q, k, v):
    b, s = q.shape[:2]
    scale = SB_HEAD_DIM ** -0.5
    key_idx = jnp.arange(s)

    def block(args):
        qb, i = args
        z = jnp.einsum('bqhd,bkhd->bhqk', qb, k) * scale
        q_idx = i * Q_BLOCK + jnp.arange(Q_BLOCK)
        strict = key_idx[None, :] < q_idx[:, None]
        log_beta = jax.nn.log_sigmoid(z)
        log_1mb = jnp.where(strict, jax.nn.log_sigmoid(-z), 0.0)
        later = lax.cumsum(log_1mb, axis=3, reverse=True) - log_1mb
        a = jnp.where(strict, jnp.exp(log_beta + later), 0.0)
        return jnp.einsum('bhqk,bkhd->bqhd', a, v)

    out = lax.map(block, (to_blocks(q), jnp.arange(s // Q_BLOCK)))
    return from_blocks(out)


def setup_inputs(seed: int = 0) -> dict:
    key = jax.random.key(seed)
    ks = jax.random.split(key, 18)
    f32 = jnp.float32

    def nrm(k, shape, scale):
        return jax.random.normal(k, shape, f32) * scale

    def gain(k, shape):
        return 1.0 + 0.02 * jax.random.normal(k, shape, f32)

    x = jax.random.normal(ks[0], (BATCH, SEQ, D_MODEL), f32)
    offset = jax.random.randint(ks[1], (BATCH, 1), 0, 1024, dtype=jnp.int32)
    positions = offset + jnp.arange(SEQ, dtype=jnp.int32)[None, :]
    return {
        "x": x,
        "positions": positions,
        "norm_attn_g": gain(ks[2], (DEPTH, D_MODEL)),
        "w_in": nrm(ks[3], (DEPTH, D_MODEL, IN_COLS), D_MODEL ** -0.5),
        "lambda_q1": nrm(ks[4], (DEPTH, HEAD_DIM), 0.1),
        "lambda_k1": nrm(ks[5], (DEPTH, HEAD_DIM), 0.1),
        "lambda_q2": nrm(ks[6], (DEPTH, HEAD_DIM), 0.1),
        "lambda_k2": nrm(ks[7], (DEPTH, HEAD_DIM), 0.1),
        "diff_subln_g": gain(ks[8], (DEPTH, DIFF_V_DIM)),
        "sb_norm_g": gain(ks[9], (DEPTH, SB_HEAD_DIM)),
        "w_out": nrm(ks[10], (DEPTH, MIX_WIDTH, D_MODEL), MIX_WIDTH ** -0.5),
        "norm_ffn_g": gain(ks[11], (DEPTH, D_MODEL)),
        "w_gate": nrm(ks[12], (DEPTH, D_MODEL, D_FF), D_MODEL ** -0.5),
        "w_up": nrm(ks[13], (DEPTH, D_MODEL, D_FF), D_MODEL ** -0.5),
        "w_down": nrm(ks[14], (DEPTH, D_FF, D_MODEL), D_FF ** -0.5),
        "final_norm_g": gain(ks[15], (D_MODEL,)),
    }


def reference(x, positions, norm_attn_g, w_in, lambda_q1, lambda_k1, lambda_q2, lambda_k2,
              diff_subln_g, sb_norm_g, w_out, norm_ffn_g, w_gate, w_up, w_down, final_norm_g):
    b, s, _ = x.shape
    f32 = jnp.float32
    for layer in range(DEPTH):
        lambda_init = 0.8 - 0.6 * math.exp(-0.3 * layer)
        h = rmsnorm(x, norm_attn_g[layer])
        proj = h @ w_in[layer]
        o1 = DIFF_QK
        o2 = o1 + DIFF_QK
        o3 = o2 + DIFF_V
        o4 = o3 + SB_QKV
        o5 = o4 + SB_QKV
        dq = proj[..., :o1].reshape(b, s, 2 * DIFF_HEADS, HEAD_DIM)
        dk = proj[..., o1:o2].reshape(b, s, 2 * DIFF_HEADS, HEAD_DIM)
        dv = proj[..., o2:o3].reshape(b, s, DIFF_HEADS, DIFF_V_DIM).astype(f32)
        dq = partial_rope(dq, positions)
        dk = partial_rope(dk, positions)
        lam = (jnp.exp(jnp.sum(lambda_q1[layer].astype(f32) * lambda_k1[layer].astype(f32)))
               - jnp.exp(jnp.sum(lambda_q2[layer].astype(f32) * lambda_k2[layer].astype(f32)))
               + lambda_init)
        d_out = differential_attention(dq, dk, dv, lam)
        d_out = rmsnorm(d_out, diff_subln_g[layer]) * (1.0 - lambda_init)
        d_out = d_out.reshape(b, s, DIFF_V).astype(x.dtype)
        sq = proj[..., o3:o4].reshape(b, s, SB_HEADS, SB_HEAD_DIM).astype(f32)
        sk = proj[..., o4:o5].reshape(b, s, SB_HEADS, SB_HEAD_DIM).astype(f32)
        sv = proj[..., o5:].reshape(b, s, SB_HEADS, SB_HEAD_DIM).astype(f32)
        s_out = stick_breaking_attention(sq, sk, sv)
        s_out = rmsnorm(s_out, sb_norm_g[layer]).reshape(b, s, SB_QKV).astype(x.dtype)
        mix = jnp.concatenate([d_out, s_out], axis=-1)
        x = x + mix @ w_out[layer]
        h2 = rmsnorm(x, norm_ffn_g[layer])
        x = x + (jax.nn.silu(h2 @ w_gate[layer]) * (h2 @ w_up[layer])) @ w_down[layer]
    return rmsnorm(x, final_norm_g)
```

```python
import functools
import math

import jax
import jax.numpy as jnp
from jax import lax
from jax.experimental import pallas as pl
from jax.experimental.pallas import tpu as pltpu

F32 = jnp.float32
BF16 = jnp.bfloat16

LANES = 128
HEAD_DIM = 64
ROPE_DIM = HEAD_DIM // 4
ROPE_HALF = ROPE_DIM // 2
ROPE_THETA = 500000.0
NORM_EPS = 1e-5
QK_SCALE = HEAD_DIM ** -0.5

VMEM_LIMIT_BYTES = 56 * 1024 * 1024

TOKEN_TILE = 512
ATTN_Q_TILE = 256
ATTN_K_TILE = 256
FFN_CHUNK = 256


def _rms_scale(x, eps=NORM_EPS):
    return lax.rsqrt(jnp.mean(x * x, axis=-1, keepdims=True) + eps)


def _dot(a, b):
    return jnp.dot(a, b, preferred_element_type=F32)


def _dot_nt(a, b):
    return lax.dot_general(a, b, (((1,), (1,)), ((), ())), preferred_element_type=F32)


def _in_proj_kernel(x_ref, pos_ref, g_ref, invf_ref, w_ref, o_ref, *, n_rope_blocks, q_blocks):
    x = x_ref[...]
    h = (x * _rms_scale(x) * g_ref[...]).astype(BF16)

    ang = pos_ref[...] * invf_ref[...]
    lane = lax.broadcasted_iota(jnp.int32, (1, LANES), 1) % HEAD_DIM
    cos = jnp.cos(ang)
    sin = jnp.sin(ang)
    c_mul = jnp.where(lane < ROPE_DIM, cos, 1.0)
    s_below = jnp.where((lane >= ROPE_HALF) & (lane < ROPE_DIM), sin, 0.0)
    s_above = jnp.where(lane < ROPE_HALF, -sin, 0.0)

    n_blocks = o_ref.shape[1] // LANES
    for blk in range(n_blocks):
        cols = slice(blk * LANES, (blk + 1) * LANES)
        t = _dot(h, w_ref[:, cols])
        if blk < n_rope_blocks:
            t = (t * c_mul
                 + pltpu.roll(t, ROPE_HALF, 1) * s_below
                 + pltpu.roll(t, LANES - ROPE_HALF, 1) * s_above)
        if blk in q_blocks:
            t = t * QK_SCALE
        o_ref[:, cols] = t.astype(BF16)


def _in_proj(x2, pos2, g, invf, w_bf, *, n_rope_blocks, q_blocks):
    n, d = x2.shape
    cols = w_bf.shape[1]
    tm = TOKEN_TILE
    kern = functools.partial(_in_proj_kernel, n_rope_blocks=n_rope_blocks, q_blocks=q_blocks)
    return pl.pallas_call(
        kern,
        out_shape=jax.ShapeDtypeStruct((n, cols), BF16),
        grid=(n // tm,),
        in_specs=[
            pl.BlockSpec((tm, d), lambda i: (i, 0)),
            pl.BlockSpec((tm, 1), lambda i: (i, 0)),
            pl.BlockSpec((1, d), lambda i: (0, 0)),
            pl.BlockSpec((1, LANES), lambda i: (0, 0)),
            pl.BlockSpec((d, cols), lambda i: (0, 0)),
        ],
        out_specs=pl.BlockSpec((tm, cols), lambda i: (i, 0)),
        compiler_params=pltpu.CompilerParams(vmem_limit_bytes=VMEM_LIMIT_BYTES),
        name="in_proj",
    )(x2, pos2, g, invf, w_bf)


def _diff_attn_kernel(lam_ref, g_ref, q_ref, k_ref, v_ref, o_ref, *, lambda_init):
    tq = q_ref.shape[0]
    tk = ATTN_K_TILE
    i = pl.program_id(2)

    lane = lax.broadcasted_iota(jnp.int32, (1, LANES), 1)
    first = lane < HEAD_DIM
    q = q_ref[...]
    zero = jnp.zeros_like(q)
    q_maps = (jnp.where(first, q, zero), jnp.where(first, zero, q))

    def scores(kb):
        return tuple(_dot_nt(qm, kb) for qm in q_maps)

    row = lax.broadcasted_iota(jnp.int32, (tq, tk), 0)
    col = lax.broadcasted_iota(jnp.int32, (tq, tk), 1)
    causal = col <= row
    d0 = pl.multiple_of(i * tq, tq)
    kb = k_ref[pl.ds(d0, tk), :]
    vb = v_ref[pl.ds(d0, tk), :]
    state = []
    for s in scores(kb):
        s = jnp.where(causal, s, -jnp.inf)
        m = jnp.max(s, axis=-1, keepdims=True)
        p = jnp.exp(s - m)
        l = jnp.sum(p, axis=-1, keepdims=True)
        acc = _dot(p.astype(BF16), vb)
        state += [m, l, acc]

    def body(j, carry):
        k0 = pl.multiple_of(j * tk, tk)
        kb = k_ref[pl.ds(k0, tk), :]
        vb = v_ref[pl.ds(k0, tk), :]
        out = []
        for idx, s in enumerate(scores(kb)):
            m, l, acc = carry[3 * idx:3 * idx + 3]
            m_new = jnp.maximum(m, jnp.max(s, axis=-1, keepdims=True))
            alpha = jnp.exp(m - m_new)
            p = jnp.exp(s - m_new)
            l = alpha * l + jnp.sum(p, axis=-1, keepdims=True)
            acc = alpha * acc + _dot(p.astype(BF16), vb)
            out += [m_new, l, acc]
        return tuple(out)

    _, l1, acc1, _, l2, acc2 = lax.fori_loop(0, i * (tq // tk), body, tuple(state))

    lam_v = lam_ref[...]
    lam = (jnp.exp(jnp.sum(lam_v[0:1] * lam_v[1:2], axis=-1, keepdims=True))
           - jnp.exp(jnp.sum(lam_v[2:3] * lam_v[3:4], axis=-1, keepdims=True))
           + lambda_init)
    o = acc1 / l1 - lam * (acc2 / l2)
    o = o * _rms_scale(o) * g_ref[...] * (1.0 - lambda_init)
    o_ref[...] = o.astype(o_ref.dtype)


def _diff_attn(proj, lam_vecs, g, *, batch, seq, heads, q_blk0, k_blk0, v_blk0, lambda_init):
    tq = ATTN_Q_TILE
    nq = seq // tq
    kern = functools.partial(_diff_attn_kernel, lambda_init=lambda_init)
    return pl.pallas_call(
        kern,
        out_shape=jax.ShapeDtypeStruct((batch * seq, heads * LANES), BF16),
        grid=(batch, heads, nq),
        in_specs=[
            pl.BlockSpec((4, HEAD_DIM), lambda b, h, i: (0, 0)),
            pl.BlockSpec((1, LANES), lambda b, h, i: (0, 0)),
            pl.BlockSpec((tq, LANES), lambda b, h, i: (b * nq + i, q_blk0 + h)),
            pl.BlockSpec((seq, LANES), lambda b, h, i: (b, k_blk0 + h)),
            pl.BlockSpec((seq, LANES), lambda b, h, i: (b, v_blk0 + h)),
        ],
        out_specs=pl.BlockSpec((tq, LANES), lambda b, h, i: (b * nq + i, h)),
        compiler_params=pltpu.CompilerParams(vmem_limit_bytes=VMEM_LIMIT_BYTES),
        name="diff_attn",
    )(lam_vecs, g, proj, proj, proj)


def _sb_attn_kernel(g_ref, q_ref, k_ref, v_ref, o_ref):
    tq = q_ref.shape[0]
    tk = ATTN_K_TILE
    i = pl.program_id(2)

    lane = lax.broadcasted_iota(jnp.int32, (1, LANES), 1)
    first = lane < HEAD_DIM
    q = q_ref[...]
    zero = jnp.zeros_like(q)
    q_heads = (jnp.where(first, q, zero), jnp.where(first, zero, q))

    rj = lax.broadcasted_iota(jnp.int32, (tk, tk), 0)
    cs = lax.broadcasted_iota(jnp.int32, (tk, tk), 1)
    tri = (rj > cs).astype(BF16)

    def block(kb, vb, carry, acc, strict):
        new_carry, new_acc = [], []
        for qh, c, a_acc in zip(q_heads, carry, acc):
            z = _dot_nt(qh, kb)
            sp = jnp.log(1.0 + jnp.exp(-jnp.abs(z)))
            log_1mb = jnp.minimum(-z, 0.0) - sp
            log_b = log_1mb + z
            if strict is not None:
                log_1mb = jnp.where(strict, log_1mb, 0.0)
            hi = log_1mb.astype(BF16)
            lo = (log_1mb - hi.astype(F32)).astype(BF16)
            later = _dot(hi, tri) + _dot(lo, tri) + c
            a = jnp.exp(log_b + later)
            if strict is not None:
                a = jnp.where(strict, a, 0.0)
            new_acc.append(a_acc + _dot(a.astype(BF16), vb))
            new_carry.append(c + jnp.sum(log_1mb, axis=-1, keepdims=True))
        return tuple(new_carry), tuple(new_acc)

    row = lax.broadcasted_iota(jnp.int32, (tq, tk), 0)
    col = lax.broadcasted_iota(jnp.int32, (tq, tk), 1)
    d0 = pl.multiple_of(i * tq, tq)
    zc = jnp.zeros((tq, 1), F32)
    za = jnp.zeros((tq, LANES), F32)
    carry, acc = block(k_ref[pl.ds(d0, tk), :], v_ref[pl.ds(d0, tk), :],
                       (zc, zc), (za, za), col < row)

    def body(step, state):
        carry, acc = state
        j = i * (tq // tk) - 1 - step
        k0 = pl.multiple_of(j * tk, tk)
        return block(k_ref[pl.ds(k0, tk), :], v_ref[pl.ds(k0, tk), :], carry, acc, None)

    _, acc = lax.fori_loop(0, i * (tq // tk), body, (carry, acc))

    o = jnp.where(first, acc[0], acc[1])
    o2 = o * o
    ss_a = jnp.sum(jnp.where(first, o2, 0.0), axis=-1, keepdims=True)
    ss_b = jnp.sum(jnp.where(first, 0.0, o2), axis=-1, keepdims=True)
    ms = jnp.where(first, ss_a, ss_b) * (1.0 / HEAD_DIM)
    o = o * lax.rsqrt(ms + NORM_EPS) * g_ref[...]
    o_ref[...] = o.astype(o_ref.dtype)


def _sb_attn(proj, g2, *, batch, seq, pairs, q_blk0, k_blk0, v_blk0):
    tq = ATTN_Q_TILE
    nq = seq // tq
    return pl.pallas_call(
        _sb_attn_kernel,
        out_shape=jax.ShapeDtypeStruct((batch * seq, pairs * LANES), BF16),
        grid=(batch, pairs, nq),
        in_specs=[
            pl.BlockSpec((1, LANES), lambda b, h, i: (0, 0)),
            pl.BlockSpec((tq, LANES), lambda b, h, i: (b * nq + i, q_blk0 + h)),
            pl.BlockSpec((seq, LANES), lambda b, h, i: (b, k_blk0 + h)),
            pl.BlockSpec((seq, LANES), lambda b, h, i: (b, v_blk0 + h)),
        ],
        out_specs=pl.BlockSpec((tq, LANES), lambda b, h, i: (b * nq + i, h)),
        compiler_params=pltpu.CompilerParams(vmem_limit_bytes=VMEM_LIMIT_BYTES),
        name="sb_attn",
    )(g2, proj, proj, proj)


def _out_ffn_kernel(x_ref, d_ref, s_ref, wo_ref, g2_ref, wg_ref, wu_ref, wd_ref, gf_ref,
                    o_ref, act_ref):
    dw = d_ref.shape[1]
    x1 = x_ref[...] + _dot(d_ref[...], wo_ref[0:dw, :]) + _dot(s_ref[...], wo_ref[dw:, :])
    h2 = (x1 * _rms_scale(x1) * g2_ref[...]).astype(BF16)
    d_ff = wg_ref.shape[1]
    for c in range(d_ff // FFN_CHUNK):
        cols = slice(c * FFN_CHUNK, (c + 1) * FFN_CHUNK)
        gate = _dot(h2, wg_ref[:, cols])
        up = _dot(h2, wu_ref[:, cols])
        act_ref[:, cols] = (gate * jax.nn.sigmoid(gate) * up).astype(BF16)
    y = x1 + _dot(act_ref[...], wd_ref[...])
    o_ref[...] = y * _rms_scale(y) * gf_ref[...]


def _out_ffn(x2, d_out, s_out, wo, g2, wg, wu, wd, gf):
    n, d = x2.shape
    d_ff = wg.shape[1]
    tm = TOKEN_TILE
    assert d_ff % FFN_CHUNK == 0
    const = lambda i: (0, 0)
    resident = functools.partial(pl.BlockSpec, index_map=const, pipeline_mode=pl.Buffered(1))
    return pl.pallas_call(
        _out_ffn_kernel,
        out_shape=jax.ShapeDtypeStruct((n, d), F32),
        grid=(n // tm,),
        in_specs=[
            pl.BlockSpec((tm, d), lambda i: (i, 0)),
            pl.BlockSpec((tm, d_out.shape[1]), lambda i: (i, 0)),
            pl.BlockSpec((tm, s_out.shape[1]), lambda i: (i, 0)),
            resident(wo.shape),
            pl.BlockSpec((1, d), const),
            resident(wg.shape),
            resident(wu.shape),
            resident(wd.shape),
            pl.BlockSpec((1, d), const),
        ],
        out_specs=pl.BlockSpec((tm, d), lambda i: (i, 0)),
        scratch_shapes=[pltpu.VMEM((tm, d_ff), BF16)],
        compiler_params=pltpu.CompilerParams(vmem_limit_bytes=VMEM_LIMIT_BYTES),
        name="out_ffn",
    )(x2, d_out, s_out, wo, g2, wg, wu, wd, gf)


def kernel(x, positions, norm_attn_g, w_in, lambda_q1, lambda_k1, lambda_q2, lambda_k2,
           diff_subln_g, sb_norm_g, w_out, norm_ffn_g, w_gate, w_up, w_down, final_norm_g):
    b, s, d = x.shape
    depth = w_in.shape[0]
    assert depth == 1
    diff_heads = d // (2 * 2 * HEAD_DIM)
    sb_pairs = d // (2 * HEAD_DIM) // 2
    assert diff_subln_g.shape[1] == LANES and sb_norm_g.shape[1] == HEAD_DIM
    assert w_in.shape[2] == 3 * diff_heads * LANES + 3 * sb_pairs * LANES
    dq0, dk0, dv0 = 0, diff_heads, 2 * diff_heads
    sq0 = 3 * diff_heads
    sk0, sv0 = sq0 + sb_pairs, sq0 + 2 * sb_pairs
    q_blocks = frozenset(range(dq0, dk0)) | frozenset(range(sq0, sk0))

    x2 = x.reshape(b * s, d)
    pos2 = positions.astype(F32).reshape(b * s, 1)
    inv_freq = ROPE_THETA ** (-jnp.arange(ROPE_HALF, dtype=F32) / ROPE_HALF)
    invf = jnp.tile(inv_freq, LANES // ROPE_HALF).reshape(1, LANES)

    layer = 0
    lambda_init = 0.8 - 0.6 * math.exp(-0.3 * layer)
    proj = _in_proj(x2, pos2, norm_attn_g[layer].reshape(1, d), invf, w_in[layer].astype(BF16),
                    n_rope_blocks=dv0, q_blocks=q_blocks)
    lam_vecs = jnp.stack([lambda_q1[layer], lambda_k1[layer], lambda_q2[layer], lambda_k2[layer]])
    d_out = _diff_attn(proj, lam_vecs.astype(F32), diff_subln_g[layer].reshape(1, LANES),
                       batch=b, seq=s, heads=diff_heads, q_blk0=dq0, k_blk0=dk0, v_blk0=dv0,
                       lambda_init=lambda_init)
    sb_g2 = jnp.tile(sb_norm_g[layer], 2).reshape(1, LANES)
    s_out = _sb_attn(proj, sb_g2, batch=b, seq=s, pairs=sb_pairs,
                     q_blk0=sq0, k_blk0=sk0, v_blk0=sv0)
    out = _out_ffn(x2, d_out, s_out, w_out[layer].astype(BF16), norm_ffn_g[layer].reshape(1, d),
                   w_gate[layer].astype(BF16), w_up[layer].astype(BF16),
                   w_down[layer].astype(BF16), final_norm_g.reshape(1, d))
    return out.reshape(b, s, d)
```

```python
import functools
import math

import jax
import jax.numpy as jnp
from jax import lax
from jax.experimental import pallas as pl
from jax.experimental.pallas import tpu as pltpu

F32 = jnp.float32
BF16 = jnp.bfloat16

LANES = 128
MXU_DIM = 256
HEAD_DIM = 64
ROPE_DIM = HEAD_DIM // 4
ROPE_HALF = ROPE_DIM // 2
ROPE_THETA = 500000.0
NORM_EPS = 1e-5
LOG2E = math.log2(math.e)
Q_SCALE = HEAD_DIM ** -0.5 * LOG2E
SOFTPLUS_CLAMP = 64.0

VMEM_LIMIT_BYTES = 56 * 1024 * 1024

TOKEN_TILE = 512
ATTN_TILE = 256
DIFF_HEADS_PER_STEP = 2
SB_PAIRS_PER_STEP = 2
FFN_CHUNK = 256


def _rms_scale(x, eps=NORM_EPS):
    return lax.rsqrt(jnp.mean(x * x, axis=-1, keepdims=True) + eps)


def _dot(a, b):
    return jnp.dot(a, b, preferred_element_type=F32)


def _dot_nt(a, b):
    return lax.dot_general(a, b, (((1,), (1,)), ((), ())), preferred_element_type=F32)


def _lane_block(ref, rows, blk):
    return ref[rows, blk * LANES:(blk + 1) * LANES]


def _split_halves(q):
    first = lax.broadcasted_iota(jnp.int32, (1, LANES), 1) < HEAD_DIM
    zero = jnp.zeros_like(q)
    return jnp.where(first, q, zero), jnp.where(first, zero, q)


def _in_proj_kernel(x_ref, pos_ref, g_ref, invf_ref, w_ref, o_ref, *, n_rope_blocks, q_blocks):
    x = x_ref[...]
    h = (x * _rms_scale(x) * g_ref[...]).astype(BF16)

    ang = pos_ref[...] * invf_ref[...]
    lane = lax.broadcasted_iota(jnp.int32, (1, LANES), 1) % HEAD_DIM
    cos = jnp.cos(ang)
    sin = jnp.sin(ang)
    c_mul = jnp.where(lane < ROPE_DIM, cos, 1.0)
    s_below = jnp.where((lane >= ROPE_HALF) & (lane < ROPE_DIM), sin, 0.0)
    s_above = jnp.where(lane < ROPE_HALF, -sin, 0.0)

    per_dot = MXU_DIM // LANES
    for c in range(o_ref.shape[1] // MXU_DIM):
        t2 = _dot(h, w_ref[:, c * MXU_DIM:(c + 1) * MXU_DIM])
        for sub in range(per_dot):
            blk = c * per_dot + sub
            t = t2[:, sub * LANES:(sub + 1) * LANES]
            if blk < n_rope_blocks:
                t = (t * c_mul
                     + pltpu.roll(t, ROPE_HALF, 1) * s_below
                     + pltpu.roll(t, LANES - ROPE_HALF, 1) * s_above)
            if blk in q_blocks:
                t = t * Q_SCALE
            o_ref[:, blk * LANES:(blk + 1) * LANES] = t.astype(BF16)


def _in_proj(x2, pos2, g, invf, w_bf, *, n_rope_blocks, q_blocks):
    n, d = x2.shape
    cols = w_bf.shape[1]
    tm = TOKEN_TILE
    kern = functools.partial(_in_proj_kernel, n_rope_blocks=n_rope_blocks, q_blocks=q_blocks)
    return pl.pallas_call(
        kern,
        out_shape=jax.ShapeDtypeStruct((n, cols), BF16),
        grid=(n // tm,),
        in_specs=[
            pl.BlockSpec((tm, d), lambda i: (i, 0)),
            pl.BlockSpec((tm, 1), lambda i: (i, 0)),
            pl.BlockSpec((1, d), lambda i: (0, 0)),
            pl.BlockSpec((1, LANES), lambda i: (0, 0)),
            pl.BlockSpec((d, cols), lambda i: (0, 0)),
        ],
        out_specs=pl.BlockSpec((tm, cols), lambda i: (i, 0)),
        compiler_params=pltpu.CompilerParams(vmem_limit_bytes=VMEM_LIMIT_BYTES),
        name="in_proj",
    )(x2, pos2, g, invf, w_bf)


def _diff_attn_kernel(lam_ref, g_ref, q_ref, k_ref, v_ref, o_ref, *, lambda_init, heads):
    t = ATTN_TILE
    i = pl.program_id(2)
    rows_all = slice(None)

    q_maps = []
    for h in range(heads):
        q_maps += list(_split_halves(_lane_block(q_ref, rows_all, h)))

    def scores(blk):
        rows = pl.ds(pl.multiple_of(blk * t, t), t)
        return [_dot_nt(qm, _lane_block(k_ref, rows, c // 2)) for c, qm in enumerate(q_maps)]

    def values(blk):
        rows = pl.ds(pl.multiple_of(blk * t, t), t)
        return [_lane_block(v_ref, rows, h) for h in range(heads)]

    row = lax.broadcasted_iota(jnp.int32, (t, t), 0)
    col = lax.broadcasted_iota(jnp.int32, (t, t), 1)
    causal = col <= row
    vbs = values(i)
    state = []
    for c, s in enumerate(scores(i)):
        s = jnp.where(causal, s, -jnp.inf)
        m = jnp.max(s, axis=-1, keepdims=True)
        p = jnp.exp2(s - m)
        l = jnp.sum(p, axis=-1, keepdims=True)
        state += [m, l, _dot(p.astype(BF16), vbs[c // 2])]

    def body(j, carry):
        vbs = values(j)
        out = []
        for c, s in enumerate(scores(j)):
            m, l, acc = carry[3 * c:3 * c + 3]
            m_new = jnp.maximum(m, jnp.max(s, axis=-1, keepdims=True))
            alpha = jnp.exp2(m - m_new)
            p = jnp.exp2(s - m_new)
            l = alpha * l + jnp.sum(p, axis=-1, keepdims=True)
            acc = alpha * acc + _dot(p.astype(BF16), vbs[c // 2])
            out += [m_new, l, acc]
        return tuple(out)

    state = lax.fori_loop(0, i, body, tuple(state))

    lam_v = lam_ref[...]
    lam = (jnp.exp(jnp.sum(lam_v[0:1] * lam_v[1:2], axis=-1, keepdims=True))
           - jnp.exp(jnp.sum(lam_v[2:3] * lam_v[3:4], axis=-1, keepdims=True))
           + lambda_init)
    for h in range(heads):
        _, l1, acc1, _, l2, acc2 = state[6 * h:6 * h + 6]
        o = acc1 / l1 - lam * (acc2 / l2)
        o = o * _rms_scale(o) * g_ref[...] * (1.0 - lambda_init)
        o_ref[:, h * LANES:(h + 1) * LANES] = o.astype(o_ref.dtype)


def _diff_attn(proj, lam_vecs, g, *, batch, seq, heads, q_blk0, k_blk0, v_blk0, lambda_init):
    t = ATTN_TILE
    nq = seq // t
    hps = DIFF_HEADS_PER_STEP
    assert heads % hps == 0 and q_blk0 % hps == 0 and k_blk0 % hps == 0 and v_blk0 % hps == 0
    kern = functools.partial(_diff_attn_kernel, lambda_init=lambda_init, heads=hps)
    w = hps * LANES
    return pl.pallas_call(
        kern,
        out_shape=jax.ShapeDtypeStruct((batch * seq, heads * LANES), BF16),
        grid=(batch, heads // hps, nq),
        in_specs=[
            pl.BlockSpec((4, HEAD_DIM), lambda b, h, i: (0, 0)),
            pl.BlockSpec((1, LANES), lambda b, h, i: (0, 0)),
            pl.BlockSpec((t, w), lambda b, h, i: (b * nq + i, q_blk0 // hps + h)),
            pl.BlockSpec((seq, w), lambda b, h, i: (b, k_blk0 // hps + h)),
            pl.BlockSpec((seq, w), lambda b, h, i: (b, v_blk0 // hps + h)),
        ],
        out_specs=pl.BlockSpec((t, w), lambda b, h, i: (b * nq + i, h)),
        compiler_params=pltpu.CompilerParams(vmem_limit_bytes=VMEM_LIMIT_BYTES),
        name="diff_attn",
    )(lam_vecs, g, proj, proj, proj)


def _sb_attn_kernel(g_ref, q_ref, k_ref, v_ref, o_ref, *, pairs):
    t = ATTN_TILE
    i = pl.program_id(2)
    rows_all = slice(None)

    q_heads = []
    for p in range(pairs):
        q_heads += list(_split_halves(_lane_block(q_ref, rows_all, p)))
    n = len(q_heads)

    rj = lax.broadcasted_iota(jnp.int32, (2 * t, t), 0) % t
    cs = lax.broadcasted_iota(jnp.int32, (2 * t, t), 1)
    tri2 = (rj >= cs).astype(BF16)

    def logits(blk):
        rows = pl.ds(pl.multiple_of(blk * t, t), t)
        return tuple(_dot_nt(qh, _lane_block(k_ref, rows, c // 2)) for c, qh in enumerate(q_heads))

    def weighted_values(a, blk):
        rows = pl.ds(pl.multiple_of(blk * t, t), t)
        return tuple(_dot(a[c], _lane_block(v_ref, rows, c // 2)) for c in range(n))

    def weights(ws, carry, strict):
        incls, new_carry = [], []
        for c in range(n):
            w = ws[c]
            sp = jnp.log(1.0 + jnp.exp2(jnp.minimum(w, SOFTPLUS_CLAMP))) * LOG2E
            neg_l = jnp.maximum(w, sp)
            if strict is not None:
                neg_l = jnp.where(strict, neg_l, 0.0)
            hi = neg_l.astype(BF16)
            lo = (neg_l - hi.astype(F32)).astype(BF16)
            incls.append(_dot(jnp.concatenate([hi, lo], axis=1), tri2))
            new_carry.append(carry[c] + jnp.sum(neg_l, axis=-1, keepdims=True))
        a_out = []
        for c in range(n):
            a = jnp.exp2((ws[c] - carry[c]) - incls[c])
            if strict is not None:
                a = jnp.where(strict, a, 0.0)
            a_out.append(a.astype(BF16))
        return tuple(new_carry), tuple(a_out)

    row = lax.broadcasted_iota(jnp.int32, (t, t), 0)
    col = lax.broadcasted_iota(jnp.int32, (t, t), 1)
    zc = jnp.zeros((t, 1), F32)
    carry, a = weights(logits(i), (zc,) * n, col < row)
    acc = weighted_values(a, i)

    def body(step, state):
        carry, acc = state
        j = i - 1 - step
        carry, a = weights(logits(j), carry, None)
        pv = weighted_values(a, j)
        return carry, tuple(acc[c] + pv[c] for c in range(n))

    _, acc = lax.fori_loop(0, i, body, (carry, acc))

    first = lax.broadcasted_iota(jnp.int32, (1, LANES), 1) < HEAD_DIM
    for p in range(pairs):
        o = jnp.where(first, acc[2 * p], acc[2 * p + 1])
        o2 = o * o
        ss_a = jnp.sum(jnp.where(first, o2, 0.0), axis=-1, keepdims=True)
        ss_b = jnp.sum(jnp.where(first, 0.0, o2), axis=-1, keepdims=True)
        ms = jnp.where(first, ss_a, ss_b) * (1.0 / HEAD_DIM)
        o = o * lax.rsqrt(ms + NORM_EPS) * g_ref[...]
        o_ref[:, p * LANES:(p + 1) * LANES] = o.astype(o_ref.dtype)


def _sb_attn(proj, g2, *, batch, seq, pairs, q_blk0, k_blk0, v_blk0):
    t = ATTN_TILE
    nq = seq // t
    pps = SB_PAIRS_PER_STEP
    assert pairs % pps == 0 and q_blk0 % pps == 0 and k_blk0 % pps == 0 and v_blk0 % pps == 0
    w = pps * LANES
    return pl.pallas_call(
        functools.partial(_sb_attn_kernel, pairs=pps),
        out_shape=jax.ShapeDtypeStruct((batch * seq, pairs * LANES), BF16),
        grid=(batch, pairs // pps, nq),
        in_specs=[
            pl.BlockSpec((1, LANES), lambda b, h, i: (0, 0)),
            pl.BlockSpec((t, w), lambda b, h, i: (b * nq + i, q_blk0 // pps + h)),
            pl.BlockSpec((seq, w), lambda b, h, i: (b, k_blk0 // pps + h)),
            pl.BlockSpec((seq, w), lambda b, h, i: (b, v_blk0 // pps + h)),
        ],
        out_specs=pl.BlockSpec((t, w), lambda b, h, i: (b * nq + i, h)),
        compiler_params=pltpu.CompilerParams(vmem_limit_bytes=VMEM_LIMIT_BYTES),
        name="sb_attn",
    )(g2, proj, proj, proj)


def _out_ffn_kernel(x_ref, d_ref, s_ref, wo_ref, g2_ref, wg_ref, wu_ref, wd_ref, gf_ref,
                    o_ref, act_ref):
    dw = d_ref.shape[1]
    x1 = x_ref[...] + _dot(d_ref[...], wo_ref[0:dw, :]) + _dot(s_ref[...], wo_ref[dw:, :])
    h2 = (x1 * _rms_scale(x1) * g2_ref[...]).astype(BF16)
    d_ff = wg_ref.shape[1]
    for c in range(d_ff // FFN_CHUNK):
        cols = slice(c * FFN_CHUNK, (c + 1) * FFN_CHUNK)
        gate = _dot(h2, wg_ref[:, cols])
        up = _dot(h2, wu_ref[:, cols])
        act_ref[:, cols] = (gate * jax.nn.sigmoid(gate) * up).astype(BF16)
    y = x1 + _dot(act_ref[...], wd_ref[...])
    o_ref[...] = y * _rms_scale(y) * gf_ref[...]


def _out_ffn(x2, d_out, s_out, wo, g2, wg, wu, wd, gf):
    n, d = x2.shape
    d_ff = wg.shape[1]
    tm = TOKEN_TILE
    assert d_ff % FFN_CHUNK == 0
    const = lambda i: (0, 0)
    resident = functools.partial(pl.BlockSpec, index_map=const, pipeline_mode=pl.Buffered(1))
    return pl.pallas_call(
        _out_ffn_kernel,
        out_shape=jax.ShapeDtypeStruct((n, d), F32),
        grid=(n // tm,),
        in_specs=[
            pl.BlockSpec((tm, d), lambda i: (i, 0)),
            pl.BlockSpec((tm, d_out.shape[1]), lambda i: (i, 0)),
            pl.BlockSpec((tm, s_out.shape[1]), lambda i: (i, 0)),
            resident(wo.shape),
            pl.BlockSpec((1, d), const),
            resident(wg.shape),
            resident(wu.shape),
            resident(wd.shape),
            pl.BlockSpec((1, d), const),
        ],
        out_specs=pl.BlockSpec((tm, d), lambda i: (i, 0)),
        scratch_shapes=[pltpu.VMEM((tm, d_ff), BF16)],
        compiler_params=pltpu.CompilerParams(vmem_limit_bytes=VMEM_LIMIT_BYTES),
        name="out_ffn",
    )(x2, d_out, s_out, wo, g2, wg, wu, wd, gf)


def kernel(x, positions, norm_attn_g, w_in, lambda_q1, lambda_k1, lambda_q2, lambda_k2,
           diff_subln_g, sb_norm_g, w_out, norm_ffn_g, w_gate, w_up, w_down, final_norm_g):
    b, s, d = x.shape
    depth = w_in.shape[0]
    assert depth == 1
    diff_heads = d // (2 * 2 * HEAD_DIM)
    sb_pairs = d // (2 * HEAD_DIM) // 2
    assert diff_subln_g.shape[1] == LANES and sb_norm_g.shape[1] == HEAD_DIM
    assert w_in.shape[2] == 3 * diff_heads * LANES + 3 * sb_pairs * LANES
    dq0, dk0, dv0 = 0, diff_heads, 2 * diff_heads
    sq0 = 3 * diff_heads
    sk0, sv0 = sq0 + sb_pairs, sq0 + 2 * sb_pairs
    q_blocks = frozenset(range(dq0, dk0)) | frozenset(range(sq0, sk0))

    x2 = x.reshape(b * s, d)
    pos2 = positions.astype(F32).reshape(b * s, 1)
    inv_freq = ROPE_THETA ** (-jnp.arange(ROPE_HALF, dtype=F32) / ROPE_HALF)
    invf = jnp.tile(inv_freq, LANES // ROPE_HALF).reshape(1, LANES)

    layer = 0
    lambda_init = 0.8 - 0.6 * math.exp(-0.3 * layer)
    proj = _in_proj(x2, pos2, norm_attn_g[layer].reshape(1, d), invf, w_in[layer].astype(BF16),
                    n_rope_blocks=dv0, q_blocks=q_blocks)
    lam_vecs = jnp.stack([lambda_q1[layer], lambda_k1[layer], lambda_q2[layer], lambda_k2[layer]])
    d_out = _diff_attn(proj, lam_vecs.astype(F32), diff_subln_g[layer].reshape(1, LANES),
                       batch=b, seq=s, heads=diff_heads, q_blk0=dq0, k_blk0=dk0, v_blk0=dv0,
                       lambda_init=lambda_init)
    sb_g2 = jnp.tile(sb_norm_g[layer], 2).reshape(1, LANES)
    s_out = _sb_attn(proj, sb_g2, batch=b, seq=s, pairs=sb_pairs,
                     q_blk0=sq0, k_blk0=sk0, v_blk0=sv0)
    out = _out_ffn(x2, d_out, s_out, w_out[layer].astype(BF16), norm_ffn_g[layer].reshape(1, d),
                   w_gate[layer].astype(BF16), w_up[layer].astype(BF16),
                   w_down[layer].astype(BF16), final_norm_g.reshape(1, d))
    return out.reshape(b, s, d)
```

```python
import functools
import math

import jax
import jax.numpy as jnp
from jax import lax
from jax.experimental import pallas as pl
from jax.experimental.pallas import tpu as pltpu

F32 = jnp.float32
BF16 = jnp.bfloat16

LANES = 128
MXU_DIM = 256
HEAD_DIM = 64
ROPE_DIM = HEAD_DIM // 4
ROPE_HALF = ROPE_DIM // 2
ROPE_THETA = 500000.0
NORM_EPS = 1e-5
LOG2E = math.log2(math.e)
Q_SCALE = HEAD_DIM ** -0.5 * LOG2E
SOFTPLUS_CLAMP = 64.0

VMEM_LIMIT_BYTES = 56 * 1024 * 1024

TOKEN_TILE = 512
ATTN_TILE = 256
DIFF_HEADS_PER_STEP = 2
SB_PAIRS_PER_STEP = 2
FFN_CHUNK = 256


def _rms_scale(x, eps=NORM_EPS):
    return lax.rsqrt(jnp.mean(x * x, axis=-1, keepdims=True) + eps)


def _dot(a, b):
    return jnp.dot(a, b, preferred_element_type=F32)


def _dot_nt(a, b):
    return lax.dot_general(a, b, (((1,), (1,)), ((), ())), preferred_element_type=F32)


def _lane_block(ref, rows, blk):
    return ref[rows, blk * LANES:(blk + 1) * LANES]


def _in_proj_kernel(x_ref, pos_ref, g_ref, invf_ref, w_ref, o_ref, *, n_rope_blocks, q_blocks):
    x = x_ref[...]
    h = (x * _rms_scale(x) * g_ref[...]).astype(BF16)

    ang = pos_ref[...] * invf_ref[...]
    lane = lax.broadcasted_iota(jnp.int32, (1, LANES), 1) % HEAD_DIM
    cos = jnp.cos(ang)
    sin = jnp.sin(ang)
    c_mul = jnp.where(lane < ROPE_DIM, cos, 1.0)
    s_below = jnp.where((lane >= ROPE_HALF) & (lane < ROPE_DIM), sin, 0.0)
    s_above = jnp.where(lane < ROPE_HALF, -sin, 0.0)

    per_dot = MXU_DIM // LANES
    for c in range(o_ref.shape[1] // MXU_DIM):
        t2 = _dot(h, w_ref[:, c * MXU_DIM:(c + 1) * MXU_DIM])
        for sub in range(per_dot):
            blk = c * per_dot + sub
            t = t2[:, sub * LANES:(sub + 1) * LANES]
            if blk < n_rope_blocks:
                t = (t * c_mul
                     + pltpu.roll(t, ROPE_HALF, 1) * s_below
                     + pltpu.roll(t, LANES - ROPE_HALF, 1) * s_above)
            if blk in q_blocks:
                t = t * Q_SCALE
            o_ref[:, blk * LANES:(blk + 1) * LANES] = t.astype(BF16)


def _in_proj(x2, pos2, g, invf, w_bf, *, n_rope_blocks, q_blocks):
    n, d = x2.shape
    cols = w_bf.shape[1]
    tm = TOKEN_TILE
    kern = functools.partial(_in_proj_kernel, n_rope_blocks=n_rope_blocks, q_blocks=q_blocks)
    return pl.pallas_call(
        kern,
        out_shape=jax.ShapeDtypeStruct((n, cols), BF16),
        grid=(n // tm,),
        in_specs=[
            pl.BlockSpec((tm, d), lambda i: (i, 0)),
            pl.BlockSpec((tm, 1), lambda i: (i, 0)),
            pl.BlockSpec((1, d), lambda i: (0, 0)),
            pl.BlockSpec((1, LANES), lambda i: (0, 0)),
            pl.BlockSpec((d, cols), lambda i: (0, 0)),
        ],
        out_specs=pl.BlockSpec((tm, cols), lambda i: (i, 0)),
        compiler_params=pltpu.CompilerParams(vmem_limit_bytes=VMEM_LIMIT_BYTES),
        name="in_proj",
    )(x2, pos2, g, invf, w_bf)


def _diff_attn_kernel(lam_ref, g_ref, q_ref, k_ref, v_ref, o_ref, s_a, s_b, m_ref, l_ref, acc_ref,
                      *, lambda_init, heads):
    t = ATTN_TILE
    i = pl.program_id(2)
    rows_all = slice(None)
    n = 2 * heads

    first = lax.broadcasted_iota(jnp.int32, (1, LANES), 1) < HEAD_DIM

    def q_map(c):
        q = _lane_block(q_ref, rows_all, c // 2)
        zero = jnp.zeros_like(q)
        return jnp.where(first, q, zero) if c % 2 == 0 else jnp.where(first, zero, q)

    def block_rows(blk):
        return pl.ds(pl.multiple_of(blk * t, t), t)

    def scores_into(s_ref, c, blk):
        s_ref[c] = _dot_nt(q_map(c), _lane_block(k_ref, block_rows(blk), c // 2))

    def softmax_stage(s_ref, c, blk, causal):
        halves = [s_ref[c, :, h * LANES:(h + 1) * LANES] for h in range(t // LANES)]
        if causal is not None:
            halves = [jnp.where(causal[:, h * LANES:(h + 1) * LANES], sh, -jnp.inf) for h, sh in enumerate(halves)]
        m_old = m_ref[c]
        m_blk = jnp.max(functools.reduce(jnp.maximum, halves), axis=-1, keepdims=True)
        m_new = jnp.maximum(m_old, m_blk)
        alpha = jnp.exp2(m_old - m_new)
        ps = [jnp.exp2(sh - m_new) for sh in halves]
        l_ref[c] = alpha * l_ref[c] + jnp.sum(functools.reduce(jnp.add, ps), axis=-1, keepdims=True)
        m_ref[c] = m_new
        p = jnp.concatenate([ph.astype(BF16) for ph in ps], axis=1)
        pv = _dot(p, _lane_block(v_ref, block_rows(blk), c // 2))
        acc_ref[c] = alpha * acc_ref[c] + pv

    def step(s_cur, s_next, blk, causal):
        nxt = jnp.maximum(blk - 1, 0)
        for c in range(n):
            scores_into(s_next, c, nxt)
        for c in range(n):
            softmax_stage(s_cur, c, blk, causal)

    row = lax.broadcasted_iota(jnp.int32, (t, t), 0)
    col = lax.broadcasted_iota(jnp.int32, (t, t), 1)
    for c in range(n):
        scores_into(s_a, c, i)
        m_ref[c] = jnp.full((t, LANES), -jnp.inf, F32)
        l_ref[c] = jnp.zeros((t, LANES), F32)
        acc_ref[c] = jnp.zeros((t, LANES), F32)
    step(s_a, s_b, i, col <= row)

    def pair_body(p, _):
        blk = i - 1 - 2 * p
        step(s_b, s_a, blk, None)
        step(s_a, s_b, blk - 1, None)
        return 0

    lax.fori_loop(0, i // 2, pair_body, 0)

    @pl.when(i % 2 == 1)
    def _():
        step(s_b, s_a, 0, None)

    lam_v = lam_ref[...]
    lam = (jnp.exp(jnp.sum(lam_v[0:1] * lam_v[1:2], axis=-1, keepdims=True))
           - jnp.exp(jnp.sum(lam_v[2:3] * lam_v[3:4], axis=-1, keepdims=True))
           + lambda_init)
    for h in range(heads):
        o = acc_ref[2 * h] / l_ref[2 * h] - lam * (acc_ref[2 * h + 1] / l_ref[2 * h + 1])
        o = o * _rms_scale(o) * g_ref[...] * (1.0 - lambda_init)
        o_ref[:, h * LANES:(h + 1) * LANES] = o.astype(o_ref.dtype)


def _diff_attn(proj, lam_vecs, g, *, batch, seq, heads, q_blk0, k_blk0, v_blk0, lambda_init):
    t = ATTN_TILE
    nq = seq // t
    hps = DIFF_HEADS_PER_STEP
    assert heads % hps == 0 and q_blk0 % hps == 0 and k_blk0 % hps == 0 and v_blk0 % hps == 0
    kern = functools.partial(_diff_attn_kernel, lambda_init=lambda_init, heads=hps)
    w = hps * LANES
    return pl.pallas_call(
        kern,
        out_shape=jax.ShapeDtypeStruct((batch * seq, heads * LANES), BF16),
        grid=(batch, heads // hps, nq),
        in_specs=[
            pl.BlockSpec((4, HEAD_DIM), lambda b, h, i: (0, 0)),
            pl.BlockSpec((1, LANES), lambda b, h, i: (0, 0)),
            pl.BlockSpec((t, w), lambda b, h, i: (b * nq + i, q_blk0 // hps + h)),
            pl.BlockSpec((seq, w), lambda b, h, i: (b, k_blk0 // hps + h)),
            pl.BlockSpec((seq, w), lambda b, h, i: (b, v_blk0 // hps + h)),
        ],
        out_specs=pl.BlockSpec((t, w), lambda b, h, i: (b * nq + i, h)),
        scratch_shapes=[
            pltpu.VMEM((2 * hps, t, t), F32), pltpu.VMEM((2 * hps, t, t), F32),
            pltpu.VMEM((2 * hps, t, LANES), F32), pltpu.VMEM((2 * hps, t, LANES), F32),
            pltpu.VMEM((2 * hps, t, LANES), F32),
        ],
        compiler_params=pltpu.CompilerParams(vmem_limit_bytes=VMEM_LIMIT_BYTES),
        name="diff_attn",
    )(lam_vecs, g, proj, proj, proj)


def _sb_attn_kernel(g_ref, q_ref, k_ref, v_ref, o_ref, ws_a, ws_b, cy_a, cy_b, a_buf, acc_ref, *, pairs):
    t = ATTN_TILE
    i = pl.program_id(2)
    rows_all = slice(None)
    n = 2 * pairs

    first = lax.broadcasted_iota(jnp.int32, (1, LANES), 1) < HEAD_DIM

    def q_head(c):
        q = _lane_block(q_ref, rows_all, c // 2)
        zero = jnp.zeros_like(q)
        return jnp.where(first, q, zero) if c % 2 == 0 else jnp.where(first, zero, q)

    rj = lax.broadcasted_iota(jnp.int32, (t, t), 0)
    cs = lax.broadcasted_iota(jnp.int32, (t, t), 1)
    tri = (rj >= cs).astype(BF16)

    def block_rows(blk):
        return pl.ds(pl.multiple_of(blk * t, t), t)

    def logits_into(ws, c, blk):
        ws[c] = _dot_nt(q_head(c), _lane_block(k_ref, block_rows(blk), c // 2))

    def values_into_acc(c, blk):
        acc_ref[c] += _dot(a_buf[c], _lane_block(v_ref, block_rows(blk), c // 2))

    def cumsum_stage(ws, cy_in, cy_out, c, strict):
        w = ws[c]
        sp = jnp.log(1.0 + jnp.exp2(jnp.minimum(w, SOFTPLUS_CLAMP))) * LOG2E
        neg_l = jnp.maximum(w, sp)
        if strict is not None:
            neg_l = jnp.where(strict, neg_l, 0.0)
        incl = _dot(neg_l.astype(BF16), tri)
        cy_out[c] = cy_in[c] + jnp.sum(neg_l, axis=-1, keepdims=True)
        return incl

    def weight_stage(ws, cy_in, c, incl, strict):
        cy = cy_in[c]
        for h in range(t // LANES):
            cols = slice(h * LANES, (h + 1) * LANES)
            a = jnp.exp2((ws[c, :, cols] - cy) - incl[:, cols])
            if strict is not None:
                a = jnp.where(strict[:, cols], a, 0.0)
            a_buf[c, :, cols] = a.astype(BF16)

    def step(ws, ws_next, cy_in, cy_out, blk, strict, prev_values):
        nxt = jnp.maximum(blk - 1, 0)
        incl = [None] * n
        for c in range(n):
            if c == 0:
                logits_into(ws_next, 0, nxt)
            if c + 1 < n:
                logits_into(ws_next, c + 1, nxt)
            incl[c] = cumsum_stage(ws, cy_in, cy_out, c, strict)
            if prev_values:
                values_into_acc(c, blk + 1)
        for c in range(n):
            weight_stage(ws, cy_in, c, incl[c], strict)

    row = lax.broadcasted_iota(jnp.int32, (t, t), 0)
    col = lax.broadcasted_iota(jnp.int32, (t, t), 1)
    for c in range(n):
        logits_into(ws_a, c, i)
        cy_a[c] = jnp.zeros((t, LANES), F32)
        acc_ref[c] = jnp.zeros((t, LANES), F32)
    step(ws_a, ws_b, cy_a, cy_b, i, col < row, False)

    def pair_body(p, _):
        blk = i - 1 - 2 * p
        step(ws_b, ws_a, cy_b, cy_a, blk, None, True)
        step(ws_a, ws_b, cy_a, cy_b, blk - 1, None, True)
        return 0

    lax.fori_loop(0, i // 2, pair_body, 0)

    @pl.when(i % 2 == 1)
    def _():
        step(ws_b, ws_a, cy_b, cy_a, 0, None, True)

    for c in range(n):
        values_into_acc(c, 0)

    for p in range(pairs):
        o = jnp.where(first, acc_ref[2 * p], acc_ref[2 * p + 1])
        o2 = o * o
        ss_a = jnp.sum(jnp.where(first, o2, 0.0), axis=-1, keepdims=True)
        ss_b = jnp.sum(jnp.where(first, 0.0, o2), axis=-1, keepdims=True)
        ms = jnp.where(first, ss_a, ss_b) * (1.0 / HEAD_DIM)
        o = o * lax.rsqrt(ms + NORM_EPS) * g_ref[...]
        o_ref[:, p * LANES:(p + 1) * LANES] = o.astype(o_ref.dtype)


def _sb_attn(proj, g2, *, batch, seq, pairs, q_blk0, k_blk0, v_blk0):
    t = ATTN_TILE
    nq = seq // t
    pps = SB_PAIRS_PER_STEP
    assert pairs % pps == 0 and q_blk0 % pps == 0 and k_blk0 % pps == 0 and v_blk0 % pps == 0
    w = pps * LANES
    return pl.pallas_call(
        functools.partial(_sb_attn_kernel, pairs=pps),
        out_shape=jax.ShapeDtypeStruct((batch * seq, pairs * LANES), BF16),
        grid=(batch, pairs // pps, nq),
        in_specs=[
            pl.BlockSpec((1, LANES), lambda b, h, i: (0, 0)),
            pl.BlockSpec((t, w), lambda b, h, i: (b * nq + i, q_blk0 // pps + h)),
            pl.BlockSpec((seq, w), lambda b, h, i: (b, k_blk0 // pps + h)),
            pl.BlockSpec((seq, w), lambda b, h, i: (b, v_blk0 // pps + h)),
        ],
        out_specs=pl.BlockSpec((t, w), lambda b, h, i: (b * nq + i, h)),
        scratch_shapes=[
            pltpu.VMEM((2 * pps, t, t), F32), pltpu.VMEM((2 * pps, t, t), F32),
            pltpu.VMEM((2 * pps, t, LANES), F32), pltpu.VMEM((2 * pps, t, LANES), F32),
            pltpu.VMEM((2 * pps, t, t), BF16),
            pltpu.VMEM((2 * pps, t, LANES), F32),
        ],
        compiler_params=pltpu.CompilerParams(vmem_limit_bytes=VMEM_LIMIT_BYTES),
        name="sb_attn",
    )(g2, proj, proj, proj)


def _out_ffn_kernel(x_ref, d_ref, s_ref, wo_ref, g2_ref, wg_ref, wu_ref, wd_ref, gf_ref,
                    o_ref, act_ref):
    dw = d_ref.shape[1]
    x1 = x_ref[...] + _dot(d_ref[...], wo_ref[0:dw, :]) + _dot(s_ref[...], wo_ref[dw:, :])
    h2 = (x1 * _rms_scale(x1) * g2_ref[...]).astype(BF16)
    d_ff = wg_ref.shape[1]
    for c in range(d_ff // FFN_CHUNK):
        cols = slice(c * FFN_CHUNK, (c + 1) * FFN_CHUNK)
        gate = _dot(h2, wg_ref[:, cols])
        up = _dot(h2, wu_ref[:, cols])
        act_ref[:, cols] = (gate * jax.nn.sigmoid(gate) * up).astype(BF16)
    y = x1 + _dot(act_ref[...], wd_ref[...])
    o_ref[...] = y * _rms_scale(y) * gf_ref[...]


def _out_ffn(x2, d_out, s_out, wo, g2, wg, wu, wd, gf):
    n, d = x2.shape
    d_ff = wg.shape[1]
    tm = TOKEN_TILE
    assert d_ff % FFN_CHUNK == 0
    const = lambda i: (0, 0)
    resident = functools.partial(pl.BlockSpec, index_map=const, pipeline_mode=pl.Buffered(1))
    return pl.pallas_call(
        _out_ffn_kernel,
        out_shape=jax.ShapeDtypeStruct((n, d), F32),
        grid=(n // tm,),
        in_specs=[
            pl.BlockSpec((tm, d), lambda i: (i, 0)),
            pl.BlockSpec((tm, d_out.shape[1]), lambda i: (i, 0)),
            pl.BlockSpec((tm, s_out.shape[1]), lambda i: (i, 0)),
            resident(wo.shape),
            pl.BlockSpec((1, d), const),
            resident(wg.shape),
            resident(wu.shape),
            resident(wd.shape),
            pl.BlockSpec((1, d), const),
        ],
        out_specs=pl.BlockSpec((tm, d), lambda i: (i, 0)),
        scratch_shapes=[pltpu.VMEM((tm, d_ff), BF16)],
        compiler_params=pltpu.CompilerParams(vmem_limit_bytes=VMEM_LIMIT_BYTES),
        name="out_ffn",
    )(x2, d_out, s_out, wo, g2, wg, wu, wd, gf)


def kernel(x, positions, norm_attn_g, w_in, lambda_q1, lambda_k1, lambda_q2, lambda_k2,
           diff_subln_g, sb_norm_g, w_out, norm_ffn_g, w_gate, w_up, w_down, final_norm_g):
    b, s, d = x.shape
    depth = w_in.shape[0]
    assert depth == 1
    diff_heads = d // (2 * 2 * HEAD_DIM)
    sb_pairs = d // (2 * HEAD_DIM) // 2
    assert diff_subln_g.shape[1] == LANES and sb_norm_g.shape[1] == HEAD_DIM
    assert w_in.shape[2] == 3 * diff_heads * LANES + 3 * sb_pairs * LANES
    dq0, dk0, dv0 = 0, diff_heads, 2 * diff_heads
    sq0 = 3 * diff_heads
    sk0, sv0 = sq0 + sb_pairs, sq0 + 2 * sb_pairs
    q_blocks = frozenset(range(dq0, dk0)) | frozenset(range(sq0, sk0))

    x2 = x.reshape(b * s, d)
    pos2 = positions.astype(F32).reshape(b * s, 1)
    inv_freq = ROPE_THETA ** (-jnp.arange(ROPE_HALF, dtype=F32) / ROPE_HALF)
    invf = jnp.tile(inv_freq, LANES // ROPE_HALF).reshape(1, LANES)

    layer = 0
    lambda_init = 0.8 - 0.6 * math.exp(-0.3 * layer)
    proj = _in_proj(x2, pos2, norm_attn_g[layer].reshape(1, d), invf, w_in[layer].astype(BF16),
                    n_rope_blocks=dv0, q_blocks=q_blocks)
    lam_vecs = jnp.stack([lambda_q1[layer], lambda_k1[layer], lambda_q2[layer], lambda_k2[layer]])
    d_out = _diff_attn(proj, lam_vecs.astype(F32), diff_subln_g[layer].reshape(1, LANES),
                       batch=b, seq=s, heads=diff_heads, q_blk0=dq0, k_blk0=dk0, v_blk0=dv0,
                       lambda_init=lambda_init)
    sb_g2 = jnp.tile(sb_norm_g[layer], 2).reshape(1, LANES)
    s_out = _sb_attn(proj, sb_g2, batch=b, seq=s, pairs=sb_pairs,
                     q_blk0=sq0, k_blk0=sk0, v_blk0=sv0)
    out = _out_ffn(x2, d_out, s_out, w_out[layer].astype(BF16), norm_ffn_g[layer].reshape(1, d),
                   w_gate[layer].astype(BF16), w_up[layer].astype(BF16),
                   w_down[layer].astype(BF16), final_norm_g.reshape(1, d))
    return out.reshape(b, s, d)
```

```python
import functools
import math

import jax
import jax.numpy as jnp
from jax import lax
from jax.experimental import pallas as pl
from jax.experimental.pallas import tpu as pltpu

F32 = jnp.float32
BF16 = jnp.bfloat16

LANES = 128
MXU_DIM = 256
HEAD_DIM = 64
ROPE_DIM = HEAD_DIM // 4
ROPE_HALF = ROPE_DIM // 2
ROPE_THETA = 500000.0
NORM_EPS = 1e-5
LOG2E = math.log2(math.e)
Q_SCALE = HEAD_DIM ** -0.5 * LOG2E
SOFTPLUS_CLAMP = 64.0

VMEM_LIMIT_BYTES = 56 * 1024 * 1024

TOKEN_TILE = 512
ATTN_TILE = 256
DIFF_HEADS_PER_STEP = 2
SB_PAIRS_PER_STEP = 2
FFN_CHUNK = 256


def _rms_scale(x, eps=NORM_EPS):
    return lax.rsqrt(jnp.mean(x * x, axis=-1, keepdims=True) + eps)


def _dot(a, b):
    return jnp.dot(a, b, preferred_element_type=F32)


def _dot_nt(a, b):
    return lax.dot_general(a, b, (((1,), (1,)), ((), ())), preferred_element_type=F32)


def _lane_block(ref, rows, blk):
    return ref[rows, blk * LANES:(blk + 1) * LANES]


def _in_proj_kernel(x_ref, pos_ref, g_ref, invf_ref, w_ref, wvt_ref, o_ref, vt_ref, *, n_rope_blocks, q_blocks):
    x = x_ref[...]
    h = (x * _rms_scale(x) * g_ref[...]).astype(BF16)

    vt_ref[...] = _dot_nt(wvt_ref[...], h).astype(BF16)

    ang = pos_ref[...] * invf_ref[...]
    lane = lax.broadcasted_iota(jnp.int32, (1, LANES), 1) % HEAD_DIM
    cos = jnp.cos(ang)
    sin = jnp.sin(ang)
    c_mul = jnp.where(lane < ROPE_DIM, cos, 1.0)
    s_below = jnp.where((lane >= ROPE_HALF) & (lane < ROPE_DIM), sin, 0.0)
    s_above = jnp.where(lane < ROPE_HALF, -sin, 0.0)

    per_dot = MXU_DIM // LANES
    for c in range(o_ref.shape[1] // MXU_DIM):
        t2 = _dot(h, w_ref[:, c * MXU_DIM:(c + 1) * MXU_DIM])
        for sub in range(per_dot):
            blk = c * per_dot + sub
            t = t2[:, sub * LANES:(sub + 1) * LANES]
            if blk < n_rope_blocks:
                t = (t * c_mul
                     + pltpu.roll(t, ROPE_HALF, 1) * s_below
                     + pltpu.roll(t, LANES - ROPE_HALF, 1) * s_above)
            if blk in q_blocks:
                t = t * Q_SCALE
            o_ref[:, blk * LANES:(blk + 1) * LANES] = t.astype(BF16)


def _in_proj(x2, pos2, g, invf, w_bf, wvt_bf, *, n_rope_blocks, q_blocks):
    n, d = x2.shape
    cols = w_bf.shape[1]
    vrows = wvt_bf.shape[0]
    tm = TOKEN_TILE
    kern = functools.partial(_in_proj_kernel, n_rope_blocks=n_rope_blocks, q_blocks=q_blocks)
    const = lambda i: (0, 0)
    return pl.pallas_call(
        kern,
        out_shape=(jax.ShapeDtypeStruct((n, cols), BF16), jax.ShapeDtypeStruct((vrows, n), BF16)),
        grid=(n // tm,),
        in_specs=[
            pl.BlockSpec((tm, d), lambda i: (i, 0)),
            pl.BlockSpec((tm, 1), lambda i: (i, 0)),
            pl.BlockSpec((1, d), const),
            pl.BlockSpec((1, LANES), const),
            pl.BlockSpec((d, cols), const),
            pl.BlockSpec((vrows, d), const),
        ],
        out_specs=(pl.BlockSpec((tm, cols), lambda i: (i, 0)), pl.BlockSpec((vrows, tm), lambda i: (0, i))),
        compiler_params=pltpu.CompilerParams(vmem_limit_bytes=VMEM_LIMIT_BYTES),
        name="in_proj",
    )(x2, pos2, g, invf, w_bf, wvt_bf)


def _diff_attn_kernel(lam_ref, g_ref, q_ref, k_ref, vt_ref, o_ref, s_a, s_b, m_ref, l_ref, acc_ref,
                      *, lambda_init, heads):
    t = ATTN_TILE
    i = pl.program_id(2)
    rows_all = slice(None)
    n = 2 * heads

    first = lax.broadcasted_iota(jnp.int32, (1, LANES), 1) < HEAD_DIM

    def q_map(c):
        q = _lane_block(q_ref, rows_all, c // 2)
        zero = jnp.zeros_like(q)
        return jnp.where(first, q, zero) if c % 2 == 0 else jnp.where(first, zero, q)

    def block_rows(blk):
        return pl.ds(pl.multiple_of(blk * t, t), t)

    def scores_into(s_ref, c, blk):
        s_ref[c] = _dot_nt(_lane_block(k_ref, block_rows(blk), c // 2), q_map(c))

    def softmax_stage(s_ref, c, blk, causal):
        s = s_ref[c]
        if causal is not None:
            s = jnp.where(causal, s, -jnp.inf)
        m_old = m_ref[c]
        m_new = jnp.maximum(m_old, jnp.max(s, axis=0, keepdims=True))
        alpha = jnp.exp2(m_old - m_new)
        p = jnp.exp2(s - m_new)
        l_ref[c] = alpha * l_ref[c] + jnp.sum(p, axis=0, keepdims=True)
        m_ref[c] = m_new
        h = c // 2
        pv = _dot(vt_ref[h * LANES:(h + 1) * LANES, block_rows(blk)], p.astype(BF16))
        acc_ref[c] = alpha * acc_ref[c] + pv

    def step(s_cur, s_next, blk, causal):
        nxt = jnp.maximum(blk - 1, 0)
        for c in range(n):
            scores_into(s_next, c, nxt)
        for c in range(n):
            softmax_stage(s_cur, c, blk, causal)

    key = lax.broadcasted_iota(jnp.int32, (t, t), 0)
    qry = lax.broadcasted_iota(jnp.int32, (t, t), 1)
    for c in range(n):
        scores_into(s_a, c, i)
        m_ref[c] = jnp.full((1, t), -jnp.inf, F32)
        l_ref[c] = jnp.zeros((1, t), F32)
        acc_ref[c] = jnp.zeros((LANES, t), F32)
    step(s_a, s_b, i, key <= qry)

    def pair_body(p, _):
        blk = i - 1 - 2 * p
        step(s_b, s_a, blk, None)
        step(s_a, s_b, blk - 1, None)
        return 0

    lax.fori_loop(0, i // 2, pair_body, 0)

    @pl.when(i % 2 == 1)
    def _():
        step(s_b, s_a, 0, None)

    lam_v = lam_ref[...]
    lam = (jnp.exp(jnp.sum(lam_v[0:1] * lam_v[1:2], axis=-1, keepdims=True))
           - jnp.exp(jnp.sum(lam_v[2:3] * lam_v[3:4], axis=-1, keepdims=True))
           + lambda_init)
    for h in range(heads):
        o_t = acc_ref[2 * h] / l_ref[2 * h] - lam * (acc_ref[2 * h + 1] / l_ref[2 * h + 1])
        o_t = o_t * lax.rsqrt(jnp.mean(o_t * o_t, axis=0, keepdims=True) + NORM_EPS)
        o = o_t.T * g_ref[...] * (1.0 - lambda_init)
        o_ref[:, h * LANES:(h + 1) * LANES] = o.astype(o_ref.dtype)


def _diff_attn(proj, v_t, lam_vecs, g, *, batch, seq, heads, q_blk0, k_blk0, vt_blk0, lambda_init):
    t = ATTN_TILE
    nq = seq // t
    hps = DIFF_HEADS_PER_STEP
    assert heads % hps == 0 and q_blk0 % hps == 0 and k_blk0 % hps == 0 and vt_blk0 % hps == 0
    kern = functools.partial(_diff_attn_kernel, lambda_init=lambda_init, heads=hps)
    w = hps * LANES
    return pl.pallas_call(
        kern,
        out_shape=jax.ShapeDtypeStruct((batch * seq, heads * LANES), BF16),
        grid=(batch, heads // hps, nq),
        in_specs=[
            pl.BlockSpec((4, HEAD_DIM), lambda b, h, i: (0, 0)),
            pl.BlockSpec((1, LANES), lambda b, h, i: (0, 0)),
            pl.BlockSpec((t, w), lambda b, h, i: (b * nq + i, q_blk0 // hps + h)),
            pl.BlockSpec((seq, w), lambda b, h, i: (b, k_blk0 // hps + h)),
            pl.BlockSpec((w, seq), lambda b, h, i: (vt_blk0 // hps + h, b)),
        ],
        out_specs=pl.BlockSpec((t, w), lambda b, h, i: (b * nq + i, h)),
        scratch_shapes=[
            pltpu.VMEM((2 * hps, t, t), F32), pltpu.VMEM((2 * hps, t, t), F32),
            pltpu.VMEM((2 * hps, 1, t), F32), pltpu.VMEM((2 * hps, 1, t), F32),
            pltpu.VMEM((2 * hps, LANES, t), F32),
        ],
        compiler_params=pltpu.CompilerParams(vmem_limit_bytes=VMEM_LIMIT_BYTES),
        name="diff_attn",
    )(lam_vecs, g, proj, proj, v_t)


def _sb_attn_kernel(g_ref, q_ref, k_ref, vt_ref, o_ref, ws_a, ws_b, cy_a, cy_b, a_buf, acc_ref, sc_ref, *, pairs):
    t = ATTN_TILE
    i = pl.program_id(2)
    rows_all = slice(None)
    n = 2 * pairs

    first = lax.broadcasted_iota(jnp.int32, (1, LANES), 1) < HEAD_DIM

    def q_head(c):
        q = _lane_block(q_ref, rows_all, c // 2)
        zero = jnp.zeros_like(q)
        return jnp.where(first, q, zero) if c % 2 == 0 else jnp.where(first, zero, q)

    si = lax.broadcasted_iota(jnp.int32, (t, t), 0)
    ji = lax.broadcasted_iota(jnp.int32, (t, t), 1)
    tri = (ji >= si).astype(BF16)

    def block_rows(blk):
        return pl.ds(pl.multiple_of(blk * t, t), t)

    def logits_into(ws, c, blk):
        ws[c] = _dot_nt(_lane_block(k_ref, block_rows(blk), c // 2), q_head(c))

    def values_into_acc(c, blk):
        v_t = vt_ref[c * HEAD_DIM:(c + 1) * HEAD_DIM, block_rows(blk)]
        acc_ref[c] += sc_ref[c] * _dot(v_t, a_buf[c])

    def cumsum_stage(ws, cy_in, cy_out, c, strict):
        w = ws[c]
        sp = jnp.log(1.0 + jnp.exp2(jnp.minimum(w, SOFTPLUS_CLAMP))) * LOG2E
        neg_l = jnp.maximum(w, sp)
        if strict is not None:
            neg_l = jnp.where(strict, neg_l, 0.0)
        incl = _dot(tri, neg_l.astype(BF16))
        cy_out[c] = cy_in[c] + incl[0:1, :]
        return incl

    def weight_stage(ws, cy_in, c, incl, strict):
        a = jnp.exp2(ws[c] - incl)
        if strict is not None:
            a = jnp.where(strict, a, 0.0)
        a_buf[c] = a.astype(BF16)
        sc_ref[c] = jnp.exp2(-cy_in[c])

    def step(ws, ws_next, cy_in, cy_out, blk, strict, prev_values):
        nxt = jnp.maximum(blk - 1, 0)
        incl = [None] * n
        for c in range(n):
            if c == 0:
                logits_into(ws_next, 0, nxt)
            if c + 1 < n:
                logits_into(ws_next, c + 1, nxt)
            incl[c] = cumsum_stage(ws, cy_in, cy_out, c, strict)
            if prev_values:
                values_into_acc(c, blk + 1)
        for c in range(n):
            weight_stage(ws, cy_in, c, incl[c], strict)

    key = lax.broadcasted_iota(jnp.int32, (t, t), 0)
    qry = lax.broadcasted_iota(jnp.int32, (t, t), 1)
    for c in range(n):
        logits_into(ws_a, c, i)
        cy_a[c] = jnp.zeros((1, t), F32)
        acc_ref[c] = jnp.zeros((HEAD_DIM, t), F32)
    step(ws_a, ws_b, cy_a, cy_b, i, key < qry, False)

    def pair_body(p, _):
        blk = i - 1 - 2 * p
        step(ws_b, ws_a, cy_b, cy_a, blk, None, True)
        step(ws_a, ws_b, cy_a, cy_b, blk - 1, None, True)
        return 0

    lax.fori_loop(0, i // 2, pair_body, 0)

    @pl.when(i % 2 == 1)
    def _():
        step(ws_b, ws_a, cy_b, cy_a, 0, None, True)

    for c in range(n):
        values_into_acc(c, 0)

    for p in range(pairs):
        halves = []
        for c in (2 * p, 2 * p + 1):
            o_t = acc_ref[c]
            halves.append(o_t * lax.rsqrt(jnp.mean(o_t * o_t, axis=0, keepdims=True) + NORM_EPS))
        o = jnp.concatenate(halves, axis=0).T * g_ref[...]
        o_ref[:, p * LANES:(p + 1) * LANES] = o.astype(o_ref.dtype)


def _sb_attn(proj, v_t, g2, *, batch, seq, pairs, q_blk0, k_blk0, vt_blk0):
    t = ATTN_TILE
    nq = seq // t
    pps = SB_PAIRS_PER_STEP
    assert pairs % pps == 0 and q_blk0 % pps == 0 and k_blk0 % pps == 0 and vt_blk0 % pps == 0
    w = pps * LANES
    return pl.pallas_call(
        functools.partial(_sb_attn_kernel, pairs=pps),
        out_shape=jax.ShapeDtypeStruct((batch * seq, pairs * LANES), BF16),
        grid=(batch, pairs // pps, nq),
        in_specs=[
            pl.BlockSpec((1, LANES), lambda b, h, i: (0, 0)),
            pl.BlockSpec((t, w), lambda b, h, i: (b * nq + i, q_blk0 // pps + h)),
            pl.BlockSpec((seq, w), lambda b, h, i: (b, k_blk0 // pps + h)),
            pl.BlockSpec((w, seq), lambda b, h, i: (vt_blk0 // pps + h, b)),
        ],
        out_specs=pl.BlockSpec((t, w), lambda b, h, i: (b * nq + i, h)),
        scratch_shapes=[
            pltpu.VMEM((2 * pps, t, t), F32), pltpu.VMEM((2 * pps, t, t), F32),
            pltpu.VMEM((2 * pps, 1, t), F32), pltpu.VMEM((2 * pps, 1, t), F32),
            pltpu.VMEM((2 * pps, t, t), BF16),
            pltpu.VMEM((2 * pps, HEAD_DIM, t), F32),
            pltpu.VMEM((2 * pps, 1, t), F32),
        ],
        compiler_params=pltpu.CompilerParams(vmem_limit_bytes=VMEM_LIMIT_BYTES),
        name="sb_attn",
    )(g2, proj, proj, v_t)


def _out_ffn_kernel(x_ref, d_ref, s_ref, wo_ref, g2_ref, wg_ref, wu_ref, wd_ref, gf_ref,
                    o_ref, act_ref):
    dw = d_ref.shape[1]
    x1 = x_ref[...] + _dot(d_ref[...], wo_ref[0:dw, :]) + _dot(s_ref[...], wo_ref[dw:, :])
    h2 = (x1 * _rms_scale(x1) * g2_ref[...]).astype(BF16)
    d_ff = wg_ref.shape[1]
    for c in range(d_ff // FFN_CHUNK):
        cols = slice(c * FFN_CHUNK, (c + 1) * FFN_CHUNK)
        gate = _dot(h2, wg_ref[:, cols])
        up = _dot(h2, wu_ref[:, cols])
        act_ref[:, cols] = (gate * jax.nn.sigmoid(gate) * up).astype(BF16)
    y = x1 + _dot(act_ref[...], wd_ref[...])
    o_ref[...] = y * _rms_scale(y) * gf_ref[...]


def _out_ffn(x2, d_out, s_out, wo, g2, wg, wu, wd, gf):
    n, d = x2.shape
    d_ff = wg.shape[1]
    tm = TOKEN_TILE
    assert d_ff % FFN_CHUNK == 0
    const = lambda i: (0, 0)
    resident = functools.partial(pl.BlockSpec, index_map=const, pipeline_mode=pl.Buffered(1))
    return pl.pallas_call(
        _out_ffn_kernel,
        out_shape=jax.ShapeDtypeStruct((n, d), F32),
        grid=(n // tm,),
        in_specs=[
            pl.BlockSpec((tm, d), lambda i: (i, 0)),
            pl.BlockSpec((tm, d_out.shape[1]), lambda i: (i, 0)),
            pl.BlockSpec((tm, s_out.shape[1]), lambda i: (i, 0)),
            resident(wo.shape),
            pl.BlockSpec((1, d), const),
            resident(wg.shape),
            resident(wu.shape),
            resident(wd.shape),
            pl.BlockSpec((1, d), const),
        ],
        out_specs=pl.BlockSpec((tm, d), lambda i: (i, 0)),
        scratch_shapes=[pltpu.VMEM((tm, d_ff), BF16)],
        compiler_params=pltpu.CompilerParams(vmem_limit_bytes=VMEM_LIMIT_BYTES),
        name="out_ffn",
    )(x2, d_out, s_out, wo, g2, wg, wu, wd, gf)


def kernel(x, positions, norm_attn_g, w_in, lambda_q1, lambda_k1, lambda_q2, lambda_k2,
           diff_subln_g, sb_norm_g, w_out, norm_ffn_g, w_gate, w_up, w_down, final_norm_g):
    b, s, d = x.shape
    depth = w_in.shape[0]
    assert depth == 1
    diff_heads = d // (2 * 2 * HEAD_DIM)
    sb_pairs = d // (2 * HEAD_DIM) // 2
    assert diff_subln_g.shape[1] == LANES and sb_norm_g.shape[1] == HEAD_DIM
    assert w_in.shape[2] == 3 * diff_heads * LANES + 3 * sb_pairs * LANES
    dq0, dk0 = 0, diff_heads
    sq0 = 2 * diff_heads
    sk0 = sq0 + sb_pairs
    q_blocks = frozenset(range(dq0, dk0)) | frozenset(range(sq0, sk0))
    dvt0, svt0 = 0, diff_heads

    x2 = x.reshape(b * s, d)
    pos2 = positions.astype(F32).reshape(b * s, 1)
    inv_freq = ROPE_THETA ** (-jnp.arange(ROPE_HALF, dtype=F32) / ROPE_HALF)
    invf = jnp.tile(inv_freq, LANES // ROPE_HALF).reshape(1, LANES)

    layer = 0
    lambda_init = 0.8 - 0.6 * math.exp(-0.3 * layer)
    qk_w = 2 * diff_heads * LANES
    v_w = diff_heads * LANES
    sb_w = sb_pairs * LANES
    w = w_in[layer].astype(BF16)
    w_rows = jnp.concatenate([w[:, :qk_w], w[:, qk_w + v_w:qk_w + v_w + 2 * sb_w]], axis=1)
    w_vt = jnp.concatenate([w[:, qk_w:qk_w + v_w], w[:, qk_w + v_w + 2 * sb_w:]], axis=1).T
    proj, v_t = _in_proj(x2, pos2, norm_attn_g[layer].reshape(1, d), invf, w_rows, w_vt,
                         n_rope_blocks=sq0, q_blocks=q_blocks)
    lam_vecs = jnp.stack([lambda_q1[layer], lambda_k1[layer], lambda_q2[layer], lambda_k2[layer]])
    d_out = _diff_attn(proj, v_t, lam_vecs.astype(F32), diff_subln_g[layer].reshape(1, LANES),
                       batch=b, seq=s, heads=diff_heads, q_blk0=dq0, k_blk0=dk0, vt_blk0=dvt0,
                       lambda_init=lambda_init)
    sb_g2 = jnp.tile(sb_norm_g[layer], 2).reshape(1, LANES)
    s_out = _sb_attn(proj, v_t, sb_g2, batch=b, seq=s, pairs=sb_pairs,
                     q_blk0=sq0, k_blk0=sk0, vt_blk0=svt0)
    out = _out_ffn(x2, d_out, s_out, w_out[layer].astype(BF16), norm_ffn_g[layer].reshape(1, d),
                   w_gate[layer].astype(BF16), w_up[layer].astype(BF16),
                   w_down[layer].astype(BF16), final_norm_g.reshape(1, d))
    return out.reshape(b, s, d)
```

```python
import functools
import math

import jax
import jax.numpy as jnp
from jax import lax
from jax.experimental import pallas as pl
from jax.experimental.pallas import tpu as pltpu

F32 = jnp.float32
BF16 = jnp.bfloat16

LANES = 128
MXU_DIM = 256
HEAD_DIM = 64
ROPE_DIM = HEAD_DIM // 4
ROPE_HALF = ROPE_DIM // 2
ROPE_THETA = 500000.0
NORM_EPS = 1e-5
LOG2E = math.log2(math.e)
Q_SCALE = HEAD_DIM ** -0.5 * LOG2E
SOFTPLUS_CLAMP = 64.0

VMEM_LIMIT_BYTES = 56 * 1024 * 1024

TOKEN_TILE = 512
ATTN_TILE = 256
DIFF_HEADS_PER_STEP = 2
SB_PAIRS_PER_STEP = 2
FFN_CHUNK = 256


def _rms_scale(x, eps=NORM_EPS):
    return lax.rsqrt(jnp.mean(x * x, axis=-1, keepdims=True) + eps)


def _dot(a, b):
    return jnp.dot(a, b, preferred_element_type=F32)


def _dot_nt(a, b):
    return lax.dot_general(a, b, (((1,), (1,)), ((), ())), preferred_element_type=F32)


def _lane_block(ref, rows, blk):
    return ref[rows, blk * LANES:(blk + 1) * LANES]


def _in_proj_kernel(x_ref, pos_ref, g_ref, invf_ref, w_ref, wvt_ref, o_ref, vt_ref, *, n_rope_blocks, q_blocks):
    x = x_ref[...]
    h = (x * _rms_scale(x) * g_ref[...]).astype(BF16)

    vt_ref[...] = _dot_nt(wvt_ref[...], h).astype(BF16)

    ang = pos_ref[...] * invf_ref[...]
    lane = lax.broadcasted_iota(jnp.int32, (1, LANES), 1) % HEAD_DIM
    cos = jnp.cos(ang)
    sin = jnp.sin(ang)
    c_mul = jnp.where(lane < ROPE_DIM, cos, 1.0)
    s_below = jnp.where((lane >= ROPE_HALF) & (lane < ROPE_DIM), sin, 0.0)
    s_above = jnp.where(lane < ROPE_HALF, -sin, 0.0)

    per_dot = MXU_DIM // LANES
    for c in range(o_ref.shape[1] // MXU_DIM):
        t2 = _dot(h, w_ref[:, c * MXU_DIM:(c + 1) * MXU_DIM])
        for sub in range(per_dot):
            blk = c * per_dot + sub
            t = t2[:, sub * LANES:(sub + 1) * LANES]
            if blk < n_rope_blocks:
                t = (t * c_mul
                     + pltpu.roll(t, ROPE_HALF, 1) * s_below
                     + pltpu.roll(t, LANES - ROPE_HALF, 1) * s_above)
            if blk in q_blocks:
                t = t * Q_SCALE
            o_ref[:, blk * LANES:(blk + 1) * LANES] = t.astype(BF16)


def _in_proj(x2, pos2, g, invf, w_bf, wvt_bf, *, n_rope_blocks, q_blocks):
    n, d = x2.shape
    cols = w_bf.shape[1]
    vrows = wvt_bf.shape[0]
    tm = TOKEN_TILE
    kern = functools.partial(_in_proj_kernel, n_rope_blocks=n_rope_blocks, q_blocks=q_blocks)
    const = lambda i: (0, 0)
    return pl.pallas_call(
        kern,
        out_shape=(jax.ShapeDtypeStruct((n, cols), BF16), jax.ShapeDtypeStruct((vrows, n), BF16)),
        grid=(n // tm,),
        in_specs=[
            pl.BlockSpec((tm, d), lambda i: (i, 0)),
            pl.BlockSpec((tm, 1), lambda i: (i, 0)),
            pl.BlockSpec((1, d), const),
            pl.BlockSpec((1, LANES), const),
            pl.BlockSpec((d, cols), const),
            pl.BlockSpec((vrows, d), const),
        ],
        out_specs=(pl.BlockSpec((tm, cols), lambda i: (i, 0)), pl.BlockSpec((vrows, tm), lambda i: (0, i))),
        compiler_params=pltpu.CompilerParams(vmem_limit_bytes=VMEM_LIMIT_BYTES),
        name="in_proj",
    )(x2, pos2, g, invf, w_bf, wvt_bf)


def _diff_attn_kernel(lam_ref, g_ref, q_ref, k_ref, vt_ref, o_ref, s_a, s_b, m_ref, l_ref, acc_ref,
                      *, lambda_init, heads):
    t = ATTN_TILE
    nq = q_ref.shape[0] // t
    n = 2 * heads

    first = lax.broadcasted_iota(jnp.int32, (1, LANES), 1) < HEAD_DIM
    key = lax.broadcasted_iota(jnp.int32, (t, t), 0)
    qry = lax.broadcasted_iota(jnp.int32, (t, t), 1)

    def block_rows(blk):
        return pl.ds(pl.multiple_of(blk * t, t), t)

    def q_map(c, i):
        q = _lane_block(q_ref, block_rows(i), c // 2)
        zero = jnp.zeros_like(q)
        return jnp.where(first, q, zero) if c % 2 == 0 else jnp.where(first, zero, q)

    def scores_into(s_ref, c, i, blk):
        s_ref[c] = _dot_nt(_lane_block(k_ref, block_rows(blk), c // 2), q_map(c, i))

    def softmax_stage(s_ref, c, blk, causal):
        s = s_ref[c]
        if causal is not None:
            s = jnp.where(causal, s, -jnp.inf)
        m_old = m_ref[c]
        m_new = jnp.maximum(m_old, jnp.max(s, axis=0, keepdims=True))
        alpha = jnp.exp2(m_old - m_new)
        p = jnp.exp2(s - m_new)
        l_ref[c] = alpha * l_ref[c] + jnp.sum(p, axis=0, keepdims=True)
        m_ref[c] = m_new
        h = c // 2
        pv = _dot(vt_ref[h * LANES:(h + 1) * LANES, block_rows(blk)], p.astype(BF16))
        acc_ref[c] = alpha * acc_ref[c] + pv

    def step(s_cur, s_next, i, blk, causal):
        nxt = jnp.maximum(blk - 1, 0)
        for c in range(n):
            scores_into(s_next, c, i, nxt)
        for c in range(n):
            softmax_stage(s_cur, c, blk, causal)

    lam_v = lam_ref[...]
    lam = (jnp.exp(jnp.sum(lam_v[0:1] * lam_v[1:2], axis=-1, keepdims=True))
           - jnp.exp(jnp.sum(lam_v[2:3] * lam_v[3:4], axis=-1, keepdims=True))
           + lambda_init)

    def query_tile(i, _):
        for c in range(n):
            m_ref[c] = jnp.full((1, t), -jnp.inf, F32)
            l_ref[c] = jnp.zeros((1, t), F32)
            acc_ref[c] = jnp.zeros((LANES, t), F32)
        step(s_a, s_b, i, i, key <= qry)

        def pair_body(p, _):
            blk = i - 1 - 2 * p
            step(s_b, s_a, i, blk, None)
            step(s_a, s_b, i, blk - 1, None)
            return 0

        lax.fori_loop(0, i // 2, pair_body, 0)

        @pl.when(i % 2 == 1)
        def _():
            step(s_b, s_a, i, 0, None)

        nxt_tile = jnp.minimum(i + 1, nq - 1)
        for c in range(n):
            scores_into(s_a, c, nxt_tile, nxt_tile)
        for h in range(heads):
            o_t = acc_ref[2 * h] / l_ref[2 * h] - lam * (acc_ref[2 * h + 1] / l_ref[2 * h + 1])
            o_t = o_t * lax.rsqrt(jnp.mean(o_t * o_t, axis=0, keepdims=True) + NORM_EPS)
            o = o_t.T * g_ref[...] * (1.0 - lambda_init)
            o_ref[block_rows(i), h * LANES:(h + 1) * LANES] = o.astype(o_ref.dtype)
        return 0

    for c in range(n):
        scores_into(s_a, c, 0, 0)
    lax.fori_loop(0, nq, query_tile, 0)


def _diff_attn(proj, v_t, lam_vecs, g, *, batch, seq, heads, q_blk0, k_blk0, vt_blk0, lambda_init):
    t = ATTN_TILE
    hps = DIFF_HEADS_PER_STEP
    assert heads % hps == 0 and q_blk0 % hps == 0 and k_blk0 % hps == 0 and vt_blk0 % hps == 0
    kern = functools.partial(_diff_attn_kernel, lambda_init=lambda_init, heads=hps)
    w = hps * LANES
    return pl.pallas_call(
        kern,
        out_shape=jax.ShapeDtypeStruct((batch * seq, heads * LANES), BF16),
        grid=(batch, heads // hps),
        in_specs=[
            pl.BlockSpec((4, HEAD_DIM), lambda b, h: (0, 0)),
            pl.BlockSpec((1, LANES), lambda b, h: (0, 0)),
            pl.BlockSpec((seq, w), lambda b, h: (b, q_blk0 // hps + h)),
            pl.BlockSpec((seq, w), lambda b, h: (b, k_blk0 // hps + h)),
            pl.BlockSpec((w, seq), lambda b, h: (vt_blk0 // hps + h, b)),
        ],
        out_specs=pl.BlockSpec((seq, w), lambda b, h: (b, h)),
        scratch_shapes=[
            pltpu.VMEM((2 * hps, t, t), F32), pltpu.VMEM((2 * hps, t, t), F32),
            pltpu.VMEM((2 * hps, 1, t), F32), pltpu.VMEM((2 * hps, 1, t), F32),
            pltpu.VMEM((2 * hps, LANES, t), F32),
        ],
        compiler_params=pltpu.CompilerParams(vmem_limit_bytes=VMEM_LIMIT_BYTES),
        name="diff_attn",
    )(lam_vecs, g, proj, proj, v_t)


def _sb_attn_kernel(g_ref, q_ref, k_ref, vt_ref, o_ref, ws_a, ws_b, cy_a, cy_b, a_buf, acc_ref, sc_ref, *, pairs):
    t = ATTN_TILE
    nq = q_ref.shape[0] // t
    n = 2 * pairs

    first = lax.broadcasted_iota(jnp.int32, (1, LANES), 1) < HEAD_DIM
    key = lax.broadcasted_iota(jnp.int32, (t, t), 0)
    qry = lax.broadcasted_iota(jnp.int32, (t, t), 1)
    tri = (qry >= key).astype(BF16)

    def block_rows(blk):
        return pl.ds(pl.multiple_of(blk * t, t), t)

    def q_head(c, i):
        q = _lane_block(q_ref, block_rows(i), c // 2)
        zero = jnp.zeros_like(q)
        return jnp.where(first, q, zero) if c % 2 == 0 else jnp.where(first, zero, q)

    def logits_into(ws, c, i, blk):
        ws[c] = _dot_nt(_lane_block(k_ref, block_rows(blk), c // 2), q_head(c, i))

    def values_into_acc(c, blk):
        v_t = vt_ref[c * HEAD_DIM:(c + 1) * HEAD_DIM, block_rows(blk)]
        acc_ref[c] += sc_ref[c] * _dot(v_t, a_buf[c])

    def cumsum_stage(ws, cy_in, cy_out, c, strict):
        w = ws[c]
        sp = jnp.log(1.0 + jnp.exp2(jnp.minimum(w, SOFTPLUS_CLAMP))) * LOG2E
        neg_l = jnp.maximum(w, sp)
        if strict is not None:
            neg_l = jnp.where(strict, neg_l, 0.0)
        incl = _dot(tri, neg_l.astype(BF16))
        cy_out[c] = cy_in[c] + incl[0:1, :]
        return incl

    def weight_stage(ws, cy_in, c, incl, strict):
        a = jnp.exp2(ws[c] - incl)
        if strict is not None:
            a = jnp.where(strict, a, 0.0)
        a_buf[c] = a.astype(BF16)
        sc_ref[c] = jnp.exp2(-cy_in[c])

    def step(ws, ws_next, cy_in, cy_out, i, blk, strict, prev_values):
        nxt = jnp.maximum(blk - 1, 0)
        incl = [None] * n
        for c in range(n):
            if c == 0:
                logits_into(ws_next, 0, i, nxt)
            if c + 1 < n:
                logits_into(ws_next, c + 1, i, nxt)
            incl[c] = cumsum_stage(ws, cy_in, cy_out, c, strict)
            if prev_values:
                values_into_acc(c, blk + 1)
        for c in range(n):
            weight_stage(ws, cy_in, c, incl[c], strict)

    def query_tile(i, _):
        for c in range(n):
            cy_a[c] = jnp.zeros((1, t), F32)
            acc_ref[c] = jnp.zeros((HEAD_DIM, t), F32)
        step(ws_a, ws_b, cy_a, cy_b, i, i, key < qry, False)

        def pair_body(p, _):
            blk = i - 1 - 2 * p
            step(ws_b, ws_a, cy_b, cy_a, i, blk, None, True)
            step(ws_a, ws_b, cy_a, cy_b, i, blk - 1, None, True)
            return 0

        lax.fori_loop(0, i // 2, pair_body, 0)

        @pl.when(i % 2 == 1)
        def _():
            step(ws_b, ws_a, cy_b, cy_a, i, 0, None, True)

        for c in range(n):
            values_into_acc(c, 0)
        nxt_tile = jnp.minimum(i + 1, nq - 1)
        for c in range(n):
            logits_into(ws_a, c, nxt_tile, nxt_tile)
        for p in range(pairs):
            halves = []
            for c in (2 * p, 2 * p + 1):
                o_t = acc_ref[c]
                halves.append(o_t * lax.rsqrt(jnp.mean(o_t * o_t, axis=0, keepdims=True) + NORM_EPS))
            o = jnp.concatenate(halves, axis=0).T * g_ref[...]
            o_ref[block_rows(i), p * LANES:(p + 1) * LANES] = o.astype(o_ref.dtype)
        return 0

    for c in range(n):
        logits_into(ws_a, c, 0, 0)
    lax.fori_loop(0, nq, query_tile, 0)


def _sb_attn(proj, v_t, g2, *, batch, seq, pairs, q_blk0, k_blk0, vt_blk0):
    t = ATTN_TILE
    pps = SB_PAIRS_PER_STEP
    assert pairs % pps == 0 and q_blk0 % pps == 0 and k_blk0 % pps == 0 and vt_blk0 % pps == 0
    w = pps * LANES
    return pl.pallas_call(
        functools.partial(_sb_attn_kernel, pairs=pps),
        out_shape=jax.ShapeDtypeStruct((batch * seq, pairs * LANES), BF16),
        grid=(batch, pairs // pps),
        in_specs=[
            pl.BlockSpec((1, LANES), lambda b, h: (0, 0)),
            pl.BlockSpec((seq, w), lambda b, h: (b, q_blk0 // pps + h)),
            pl.BlockSpec((seq, w), lambda b, h: (b, k_blk0 // pps + h)),
            pl.BlockSpec((w, seq), lambda b, h: (vt_blk0 // pps + h, b)),
        ],
        out_specs=pl.BlockSpec((seq, w), lambda b, h: (b, h)),
        scratch_shapes=[
            pltpu.VMEM((2 * pps, t, t), F32), pltpu.VMEM((2 * pps, t, t), F32),
            pltpu.VMEM((2 * pps, 1, t), F32), pltpu.VMEM((2 * pps, 1, t), F32),
            pltpu.VMEM((2 * pps, t, t), BF16),
            pltpu.VMEM((2 * pps, HEAD_DIM, t), F32),
            pltpu.VMEM((2 * pps, 1, t), F32),
        ],
        compiler_params=pltpu.CompilerParams(vmem_limit_bytes=VMEM_LIMIT_BYTES),
        name="sb_attn",
    )(g2, proj, proj, v_t)


def _out_ffn_kernel(x_ref, d_ref, s_ref, wo_ref, g2_ref, wg_ref, wu_ref, wd_ref, gf_ref,
                    o_ref, act_ref):
    dw = d_ref.shape[1]
    x1 = x_ref[...] + _dot(d_ref[...], wo_ref[0:dw, :]) + _dot(s_ref[...], wo_ref[dw:, :])
    h2 = (x1 * _rms_scale(x1) * g2_ref[...]).astype(BF16)
    d_ff = wg_ref.shape[1]
    for c in range(d_ff // FFN_CHUNK):
        cols = slice(c * FFN_CHUNK, (c + 1) * FFN_CHUNK)
        gate = _dot(h2, wg_ref[:, cols])
        up = _dot(h2, wu_ref[:, cols])
        act_ref[:, cols] = (gate * jax.nn.sigmoid(gate) * up).astype(BF16)
    y = x1 + _dot(act_ref[...], wd_ref[...])
    o_ref[...] = y * _rms_scale(y) * gf_ref[...]


def _out_ffn(x2, d_out, s_out, wo, g2, wg, wu, wd, gf):
    n, d = x2.shape
    d_ff = wg.shape[1]
    tm = TOKEN_TILE
    assert d_ff % FFN_CHUNK == 0
    const = lambda i: (0, 0)
    resident = functools.partial(pl.BlockSpec, index_map=const, pipeline_mode=pl.Buffered(1))
    return pl.pallas_call(
        _out_ffn_kernel,
        out_shape=jax.ShapeDtypeStruct((n, d), F32),
        grid=(n // tm,),
        in_specs=[
            pl.BlockSpec((tm, d), lambda i: (i, 0)),
            pl.BlockSpec((tm, d_out.shape[1]), lambda i: (i, 0)),
            pl.BlockSpec((tm, s_out.shape[1]), lambda i: (i, 0)),
            resident(wo.shape),
            pl.BlockSpec((1, d), const),
            resident(wg.shape),
            resident(wu.shape),
            resident(wd.shape),
            pl.BlockSpec((1, d), const),
        ],
        out_specs=pl.BlockSpec((tm, d), lambda i: (i, 0)),
        scratch_shapes=[pltpu.VMEM((tm, d_ff), BF16)],
        compiler_params=pltpu.CompilerParams(vmem_limit_bytes=VMEM_LIMIT_BYTES),
        name="out_ffn",
    )(x2, d_out, s_out, wo, g2, wg, wu, wd, gf)


def kernel(x, positions, norm_attn_g, w_in, lambda_q1, lambda_k1, lambda_q2, lambda_k2,
           diff_subln_g, sb_norm_g, w_out, norm_ffn_g, w_gate, w_up, w_down, final_norm_g):
    b, s, d = x.shape
    depth = w_in.shape[0]
    assert depth == 1
    diff_heads = d // (2 * 2 * HEAD_DIM)
    sb_pairs = d // (2 * HEAD_DIM) // 2
    assert diff_subln_g.shape[1] == LANES and sb_norm_g.shape[1] == HEAD_DIM
    assert w_in.shape[2] == 3 * diff_heads * LANES + 3 * sb_pairs * LANES
    dq0, dk0 = 0, diff_heads
    sq0 = 2 * diff_heads
    sk0 = sq0 + sb_pairs
    q_blocks = frozenset(range(dq0, dk0)) | frozenset(range(sq0, sk0))
    dvt0, svt0 = 0, diff_heads

    x2 = x.reshape(b * s, d)
    pos2 = positions.astype(F32).reshape(b * s, 1)
    inv_freq = ROPE_THETA ** (-jnp.arange(ROPE_HALF, dtype=F32) / ROPE_HALF)
    invf = jnp.tile(inv_freq, LANES // ROPE_HALF).reshape(1, LANES)

    layer = 0
    lambda_init = 0.8 - 0.6 * math.exp(-0.3 * layer)
    qk_w = 2 * diff_heads * LANES
    v_w = diff_heads * LANES
    sb_w = sb_pairs * LANES
    w = w_in[layer].astype(BF16)
    w_rows = jnp.concatenate([w[:, :qk_w], w[:, qk_w + v_w:qk_w + v_w + 2 * sb_w]], axis=1)
    w_vt = jnp.concatenate([w[:, qk_w:qk_w + v_w], w[:, qk_w + v_w + 2 * sb_w:]], axis=1).T
    proj, v_t = _in_proj(x2, pos2, norm_attn_g[layer].reshape(1, d), invf, w_rows, w_vt,
                         n_rope_blocks=sq0, q_blocks=q_blocks)
    lam_vecs = jnp.stack([lambda_q1[layer], lambda_k1[layer], lambda_q2[layer], lambda_k2[layer]])
    d_out = _diff_attn(proj, v_t, lam_vecs.astype(F32), diff_subln_g[layer].reshape(1, LANES),
                       batch=b, seq=s, heads=diff_heads, q_blk0=dq0, k_blk0=dk0, vt_blk0=dvt0,
                       lambda_init=lambda_init)
    sb_g2 = jnp.tile(sb_norm_g[layer], 2).reshape(1, LANES)
    s_out = _sb_attn(proj, v_t, sb_g2, batch=b, seq=s, pairs=sb_pairs,
                     q_blk0=sq0, k_blk0=sk0, vt_blk0=svt0)
    out = _out_ffn(x2, d_out, s_out, w_out[layer].astype(BF16), norm_ffn_g[layer].reshape(1, d),
                   w_gate[layer].astype(BF16), w_up[layer].astype(BF16),
                   w_down[layer].astype(BF16), final_norm_g.reshape(1, d))
    return out.reshape(b, s, d)
```

```python
import functools
import math

import jax
import jax.numpy as jnp
from jax import lax
from jax.experimental import pallas as pl
from jax.experimental.pallas import tpu as pltpu

F32 = jnp.float32
BF16 = jnp.bfloat16

LANES = 128
MXU_DIM = 256
HEAD_DIM = 64
ROPE_DIM = HEAD_DIM // 4
ROPE_HALF = ROPE_DIM // 2
ROPE_THETA = 500000.0
NORM_EPS = 1e-5
LOG2E = math.log2(math.e)
Q_SCALE = HEAD_DIM ** -0.5 * LOG2E
SOFTPLUS_CLAMP = 64.0

VMEM_LIMIT_BYTES = 56 * 1024 * 1024

TOKEN_TILE = 512
ATTN_TILE = 256
DIFF_HEADS_PER_STEP = 2
SB_PAIRS_PER_STEP = 2
FFN_CHUNK = 256


def _rms_scale(x, eps=NORM_EPS):
    return lax.rsqrt(jnp.mean(x * x, axis=-1, keepdims=True) + eps)


def _dot(a, b):
    return jnp.dot(a, b, preferred_element_type=F32)


def _dot_nt(a, b):
    return lax.dot_general(a, b, (((1,), (1,)), ((), ())), preferred_element_type=F32)


def _lane_block(ref, rows, blk):
    return ref[rows, blk * LANES:(blk + 1) * LANES]


def _in_proj_kernel(x_ref, pos_ref, g_ref, invf_ref, w_ref, wvt_ref, o_ref, vt_ref, *, n_rope_blocks, q_blocks):
    x = x_ref[...]
    h = (x * _rms_scale(x) * g_ref[...]).astype(BF16)

    vt_ref[...] = _dot_nt(wvt_ref[...], h).astype(BF16)

    ang = invf_ref[...] * pos_ref[...]
    reps = LANES // ROPE_HALF
    cos = jnp.tile(jnp.cos(ang), (reps, 1)).T
    sin = jnp.tile(jnp.sin(ang), (reps, 1)).T
    lane = lax.broadcasted_iota(jnp.int32, (1, LANES), 1) % HEAD_DIM
    c_mul = jnp.where(lane < ROPE_DIM, cos, 1.0)
    s_below = jnp.where((lane >= ROPE_HALF) & (lane < ROPE_DIM), sin, 0.0)
    s_above = jnp.where(lane < ROPE_HALF, -sin, 0.0)

    per_dot = MXU_DIM // LANES
    for c in range(o_ref.shape[1] // MXU_DIM):
        t2 = _dot(h, w_ref[:, c * MXU_DIM:(c + 1) * MXU_DIM])
        for sub in range(per_dot):
            blk = c * per_dot + sub
            t = t2[:, sub * LANES:(sub + 1) * LANES]
            if blk < n_rope_blocks:
                t = (t * c_mul
                     + pltpu.roll(t, ROPE_HALF, 1) * s_below
                     + pltpu.roll(t, LANES - ROPE_HALF, 1) * s_above)
            if blk in q_blocks:
                t = t * Q_SCALE
            o_ref[:, blk * LANES:(blk + 1) * LANES] = t.astype(BF16)


def _in_proj(x2, pos2, g, invf, w_bf, wvt_bf, *, n_rope_blocks, q_blocks):
    n, d = x2.shape
    cols = w_bf.shape[1]
    vrows = wvt_bf.shape[0]
    tm = TOKEN_TILE
    kern = functools.partial(_in_proj_kernel, n_rope_blocks=n_rope_blocks, q_blocks=q_blocks)
    const = lambda i: (0, 0)
    return pl.pallas_call(
        kern,
        out_shape=(jax.ShapeDtypeStruct((n, cols), BF16), jax.ShapeDtypeStruct((vrows, n), BF16)),
        grid=(n // tm,),
        in_specs=[
            pl.BlockSpec((tm, d), lambda i: (i, 0)),
            pl.BlockSpec((None, 1, tm), lambda i: (i, 0, 0)),
            pl.BlockSpec((1, d), const),
            pl.BlockSpec((ROPE_HALF, 1), const),
            pl.BlockSpec((d, cols), const),
            pl.BlockSpec((vrows, d), const),
        ],
        out_specs=(pl.BlockSpec((tm, cols), lambda i: (i, 0)), pl.BlockSpec((vrows, tm), lambda i: (0, i))),
        compiler_params=pltpu.CompilerParams(vmem_limit_bytes=VMEM_LIMIT_BYTES),
        name="in_proj",
    )(x2, pos2, g, invf, w_bf, wvt_bf)


def _diff_attn_kernel(lam_ref, g_ref, q_ref, k_ref, vt_ref, o_ref, s_a, s_b, m_ref, l_ref, acc_ref,
                      *, lambda_init, heads):
    t = ATTN_TILE
    nq = q_ref.shape[0] // t
    n = 2 * heads

    first = lax.broadcasted_iota(jnp.int32, (1, LANES), 1) < HEAD_DIM
    key = lax.broadcasted_iota(jnp.int32, (t, t), 0)
    qry = lax.broadcasted_iota(jnp.int32, (t, t), 1)

    def block_rows(blk):
        return pl.ds(pl.multiple_of(blk * t, t), t)

    def q_map(c, i):
        q = _lane_block(q_ref, block_rows(i), c // 2)
        zero = jnp.zeros_like(q)
        return jnp.where(first, q, zero) if c % 2 == 0 else jnp.where(first, zero, q)

    def scores_into(s_ref, c, i, blk):
        s_ref[c] = _dot_nt(_lane_block(k_ref, block_rows(blk), c // 2), q_map(c, i))

    def softmax_stage(s_ref, c, blk, causal):
        s = s_ref[c]
        if causal is not None:
            s = jnp.where(causal, s, -jnp.inf)
        m_old = m_ref[c]
        m_new = jnp.maximum(m_old, jnp.max(s, axis=0, keepdims=True))
        alpha = jnp.exp2(m_old - m_new)
        p = jnp.exp2(s - m_new)
        l_ref[c] = alpha * l_ref[c] + jnp.sum(p, axis=0, keepdims=True)
        m_ref[c] = m_new
        h = c // 2
        pv = _dot(vt_ref[h * LANES:(h + 1) * LANES, block_rows(blk)], p.astype(BF16))
        acc_ref[c] = alpha * acc_ref[c] + pv

    def step(s_cur, s_next, i, blk, causal, prefetch=True):
        if prefetch:
            nxt = jnp.maximum(blk - 1, 0)
            for c in range(n):
                scores_into(s_next, c, i, nxt)
        for c in range(n):
            softmax_stage(s_cur, c, blk, causal)

    lam_v = lam_ref[...]
    lam = (jnp.exp(jnp.sum(lam_v[0:1] * lam_v[1:2], axis=-1, keepdims=True))
           - jnp.exp(jnp.sum(lam_v[2:3] * lam_v[3:4], axis=-1, keepdims=True))
           + lambda_init)

    def query_tile(i, _):
        for c in range(n):
            m_ref[c] = jnp.full((1, t), -jnp.inf, F32)
            l_ref[c] = jnp.zeros((1, t), F32)
            acc_ref[c] = jnp.zeros((LANES, t), F32)
        step(s_a, s_b, i, i, key <= qry)

        def pair_body(p, _):
            blk = i - 1 - 2 * p
            step(s_b, s_a, i, blk, None)
            step(s_a, s_b, i, blk - 1, None)
            return 0

        lax.fori_loop(0, jnp.maximum(i - 1, 0) // 2, pair_body, 0)

        @pl.when(i % 2 == 1)
        def _():
            step(s_b, s_a, i, 0, None, prefetch=False)

        @pl.when((i % 2 == 0) & (i >= 2))
        def _():
            step(s_b, s_a, i, 1, None)
            step(s_a, s_b, i, 0, None, prefetch=False)

        nxt_tile = jnp.minimum(i + 1, nq - 1)
        for c in range(n):
            scores_into(s_a, c, nxt_tile, nxt_tile)
        for h in range(heads):
            o_t = acc_ref[2 * h] / l_ref[2 * h] - lam * (acc_ref[2 * h + 1] / l_ref[2 * h + 1])
            o_t = o_t * lax.rsqrt(jnp.mean(o_t * o_t, axis=0, keepdims=True) + NORM_EPS)
            o = o_t.T * g_ref[...] * (1.0 - lambda_init)
            o_ref[block_rows(i), h * LANES:(h + 1) * LANES] = o.astype(o_ref.dtype)
        return 0

    for c in range(n):
        scores_into(s_a, c, 0, 0)
    lax.fori_loop(0, nq, query_tile, 0)


def _diff_attn(proj, v_t, lam_vecs, g, *, batch, seq, heads, q_blk0, k_blk0, vt_blk0, lambda_init):
    t = ATTN_TILE
    hps = DIFF_HEADS_PER_STEP
    assert heads % hps == 0 and q_blk0 % hps == 0 and k_blk0 % hps == 0 and vt_blk0 % hps == 0
    kern = functools.partial(_diff_attn_kernel, lambda_init=lambda_init, heads=hps)
    w = hps * LANES
    return pl.pallas_call(
        kern,
        out_shape=jax.ShapeDtypeStruct((batch * seq, heads * LANES), BF16),
        grid=(batch, heads // hps),
        in_specs=[
            pl.BlockSpec((4, HEAD_DIM), lambda b, h: (0, 0)),
            pl.BlockSpec((1, LANES), lambda b, h: (0, 0)),
            pl.BlockSpec((seq, w), lambda b, h: (b, q_blk0 // hps + h)),
            pl.BlockSpec((seq, w), lambda b, h: (b, k_blk0 // hps + h)),
            pl.BlockSpec((w, seq), lambda b, h: (vt_blk0 // hps + h, b)),
        ],
        out_specs=pl.BlockSpec((seq, w), lambda b, h: (b, h)),
        scratch_shapes=[
            pltpu.VMEM((2 * hps, t, t), F32), pltpu.VMEM((2 * hps, t, t), F32),
            pltpu.VMEM((2 * hps, 1, t), F32), pltpu.VMEM((2 * hps, 1, t), F32),
            pltpu.VMEM((2 * hps, LANES, t), F32),
        ],
        compiler_params=pltpu.CompilerParams(vmem_limit_bytes=VMEM_LIMIT_BYTES),
        name="diff_attn",
    )(lam_vecs, g, proj, proj, v_t)


def _sb_attn_kernel(g_ref, q_ref, k_ref, vt_ref, o_ref, ws_a, ws_b, cy_a, cy_b, a_buf, acc_ref, sc_ref, *, pairs):
    t = ATTN_TILE
    nq = q_ref.shape[0] // t
    n = 2 * pairs

    first = lax.broadcasted_iota(jnp.int32, (1, LANES), 1) < HEAD_DIM
    key = lax.broadcasted_iota(jnp.int32, (t, t), 0)
    qry = lax.broadcasted_iota(jnp.int32, (t, t), 1)
    tri = (qry >= key).astype(BF16)

    def block_rows(blk):
        return pl.ds(pl.multiple_of(blk * t, t), t)

    def q_head(c, i):
        q = _lane_block(q_ref, block_rows(i), c // 2)
        zero = jnp.zeros_like(q)
        return jnp.where(first, q, zero) if c % 2 == 0 else jnp.where(first, zero, q)

    def logits_into(ws, c, i, blk):
        ws[c] = _dot_nt(_lane_block(k_ref, block_rows(blk), c // 2), q_head(c, i))

    def values_into_acc(c, blk):
        v_t = vt_ref[c * HEAD_DIM:(c + 1) * HEAD_DIM, block_rows(blk)]
        acc_ref[c] += sc_ref[c] * _dot(v_t, a_buf[c])

    def cumsum_stage(ws, cy_in, cy_out, c, strict):
        w = ws[c]
        sp = jnp.log(1.0 + jnp.exp2(jnp.minimum(w, SOFTPLUS_CLAMP))) * LOG2E
        neg_l = jnp.maximum(w, sp)
        if strict is not None:
            neg_l = jnp.where(strict, neg_l, 0.0)
        incl = _dot(tri, neg_l.astype(BF16))
        cy_out[c] = cy_in[c] + incl[0:1, :]
        return incl

    def weight_stage(ws, cy_in, c, incl, strict):
        a = jnp.exp2(ws[c] - incl)
        if strict is not None:
            a = jnp.where(strict, a, 0.0)
        a_buf[c] = a.astype(BF16)
        sc_ref[c] = jnp.exp2(-cy_in[c])

    def step(ws, ws_next, cy_in, cy_out, i, blk, strict, prev_values, prefetch=True):
        nxt = jnp.maximum(blk - 1, 0)
        incl = [None] * n
        for c in range(n):
            if prefetch and c == 0:
                logits_into(ws_next, 0, i, nxt)
            if prefetch and c + 1 < n:
                logits_into(ws_next, c + 1, i, nxt)
            incl[c] = cumsum_stage(ws, cy_in, cy_out, c, strict)
            if prev_values:
                values_into_acc(c, blk + 1)
        for c in range(n):
            weight_stage(ws, cy_in, c, incl[c], strict)

    def query_tile(i, _):
        for c in range(n):
            cy_a[c] = jnp.zeros((1, t), F32)
            acc_ref[c] = jnp.zeros((HEAD_DIM, t), F32)
        step(ws_a, ws_b, cy_a, cy_b, i, i, key < qry, False)

        def pair_body(p, _):
            blk = i - 1 - 2 * p
            step(ws_b, ws_a, cy_b, cy_a, i, blk, None, True)
            step(ws_a, ws_b, cy_a, cy_b, i, blk - 1, None, True)
            return 0

        lax.fori_loop(0, jnp.maximum(i - 1, 0) // 2, pair_body, 0)

        @pl.when(i % 2 == 1)
        def _():
            step(ws_b, ws_a, cy_b, cy_a, i, 0, None, True, prefetch=False)

        @pl.when((i % 2 == 0) & (i >= 2))
        def _():
            step(ws_b, ws_a, cy_b, cy_a, i, 1, None, True)
            step(ws_a, ws_b, cy_a, cy_b, i, 0, None, True, prefetch=False)

        for c in range(n):
            values_into_acc(c, 0)
        nxt_tile = jnp.minimum(i + 1, nq - 1)
        for c in range(n):
            logits_into(ws_a, c, nxt_tile, nxt_tile)
        for p in range(pairs):
            halves = []
            for c in (2 * p, 2 * p + 1):
                o_t = acc_ref[c]
                halves.append(o_t * lax.rsqrt(jnp.mean(o_t * o_t, axis=0, keepdims=True) + NORM_EPS))
            o = jnp.concatenate(halves, axis=0).T * g_ref[...]
            o_ref[block_rows(i), p * LANES:(p + 1) * LANES] = o.astype(o_ref.dtype)
        return 0

    for c in range(n):
        logits_into(ws_a, c, 0, 0)
    lax.fori_loop(0, nq, query_tile, 0)


def _sb_attn(proj, v_t, g2, *, batch, seq, pairs, q_blk0, k_blk0, vt_blk0):
    t = ATTN_TILE
    pps = SB_PAIRS_PER_STEP
    assert pairs % pps == 0 and q_blk0 % pps == 0 and k_blk0 % pps == 0 and vt_blk0 % pps == 0
    w = pps * LANES
    return pl.pallas_call(
        functools.partial(_sb_attn_kernel, pairs=pps),
        out_shape=jax.ShapeDtypeStruct((batch * seq, pairs * LANES), BF16),
        grid=(batch, pairs // pps),
        in_specs=[
            pl.BlockSpec((1, LANES), lambda b, h: (0, 0)),
            pl.BlockSpec((seq, w), lambda b, h: (b, q_blk0 // pps + h)),
            pl.BlockSpec((seq, w), lambda b, h: (b, k_blk0 // pps + h)),
            pl.BlockSpec((w, seq), lambda b, h: (vt_blk0 // pps + h, b)),
        ],
        out_specs=pl.BlockSpec((seq, w), lambda b, h: (b, h)),
        scratch_shapes=[
            pltpu.VMEM((2 * pps, t, t), F32), pltpu.VMEM((2 * pps, t, t), F32),
            pltpu.VMEM((2 * pps, 1, t), F32), pltpu.VMEM((2 * pps, 1, t), F32),
            pltpu.VMEM((2 * pps, t, t), BF16),
            pltpu.VMEM((2 * pps, HEAD_DIM, t), F32),
            pltpu.VMEM((2 * pps, 1, t), F32),
        ],
        compiler_params=pltpu.CompilerParams(vmem_limit_bytes=VMEM_LIMIT_BYTES),
        name="sb_attn",
    )(g2, proj, proj, v_t)


def _out_ffn_kernel(x_ref, d_ref, s_ref, wo_ref, g2_ref, wg_ref, wu_ref, wd_ref, gf_ref,
                    o_ref, act_ref):
    dw = d_ref.shape[1]
    x1 = x_ref[...] + _dot(d_ref[...], wo_ref[0:dw, :]) + _dot(s_ref[...], wo_ref[dw:, :])
    h2 = (x1 * _rms_scale(x1) * g2_ref[...]).astype(BF16)
    d_ff = wg_ref.shape[1]
    for c in range(d_ff // FFN_CHUNK):
        cols = slice(c * FFN_CHUNK, (c + 1) * FFN_CHUNK)
        gate = _dot(h2, wg_ref[:, cols])
        up = _dot(h2, wu_ref[:, cols])
        act_ref[:, cols] = (gate * jax.nn.sigmoid(gate) * up).astype(BF16)
    y = x1 + _dot(act_ref[...], wd_ref[...])
    o_ref[...] = y * _rms_scale(y) * gf_ref[...]


def _out_ffn(x2, d_out, s_out, wo, g2, wg, wu, wd, gf):
    n, d = x2.shape
    d_ff = wg.shape[1]
    tm = TOKEN_TILE
    assert d_ff % FFN_CHUNK == 0
    const = lambda i: (0, 0)
    resident = functools.partial(pl.BlockSpec, index_map=const, pipeline_mode=pl.Buffered(1))
    return pl.pallas_call(
        _out_ffn_kernel,
        out_shape=jax.ShapeDtypeStruct((n, d), F32),
        grid=(n // tm,),
        in_specs=[
            pl.BlockSpec((tm, d), lambda i: (i, 0)),
            pl.BlockSpec((tm, d_out.shape[1]), lambda i: (i, 0)),
            pl.BlockSpec((tm, s_out.shape[1]), lambda i: (i, 0)),
            resident(wo.shape),
            pl.BlockSpec((1, d), const),
            resident(wg.shape),
            resident(wu.shape),
            resident(wd.shape),
            pl.BlockSpec((1, d), const),
        ],
        out_specs=pl.BlockSpec((tm, d), lambda i: (i, 0)),
        scratch_shapes=[pltpu.VMEM((tm, d_ff), BF16)],
        compiler_params=pltpu.CompilerParams(vmem_limit_bytes=VMEM_LIMIT_BYTES),
        name="out_ffn",
    )(x2, d_out, s_out, wo, g2, wg, wu, wd, gf)


def kernel(x, positions, norm_attn_g, w_in, lambda_q1, lambda_k1, lambda_q2, lambda_k2,
           diff_subln_g, sb_norm_g, w_out, norm_ffn_g, w_gate, w_up, w_down, final_norm_g):
    b, s, d = x.shape
    depth = w_in.shape[0]
    assert depth == 1
    diff_heads = d // (2 * 2 * HEAD_DIM)
    sb_pairs = d // (2 * HEAD_DIM) // 2
    assert diff_subln_g.shape[1] == LANES and sb_norm_g.shape[1] == HEAD_DIM
    assert w_in.shape[2] == 3 * diff_heads * LANES + 3 * sb_pairs * LANES
    dq0, dk0 = 0, diff_heads
    sq0 = 2 * diff_heads
    sk0 = sq0 + sb_pairs
    q_blocks = frozenset(range(dq0, dk0)) | frozenset(range(sq0, sk0))
    dvt0, svt0 = 0, diff_heads

    x2 = x.reshape(b * s, d)
    pos2 = positions.astype(F32).reshape(b * s // TOKEN_TILE, 1, TOKEN_TILE)
    inv_freq = ROPE_THETA ** (-jnp.arange(ROPE_HALF, dtype=F32) / ROPE_HALF)
    invf = inv_freq.reshape(ROPE_HALF, 1)

    layer = 0
    lambda_init = 0.8 - 0.6 * math.exp(-0.3 * layer)
    qk_w = 2 * diff_heads * LANES
    v_w = diff_heads * LANES
    sb_w = sb_pairs * LANES
    w = w_in[layer].astype(BF16)
    w_rows = jnp.concatenate([w[:, :qk_w], w[:, qk_w + v_w:qk_w + v_w + 2 * sb_w]], axis=1)
    w_vt = jnp.concatenate([w[:, qk_w:qk_w + v_w], w[:, qk_w + v_w + 2 * sb_w:]], axis=1).T
    proj, v_t = _in_proj(x2, pos2, norm_attn_g[layer].reshape(1, d), invf, w_rows, w_vt,
                         n_rope_blocks=sq0, q_blocks=q_blocks)
    lam_vecs = jnp.stack([lambda_q1[layer], lambda_k1[layer], lambda_q2[layer], lambda_k2[layer]])
    d_out = _diff_attn(proj, v_t, lam_vecs.astype(F32), diff_subln_g[layer].reshape(1, LANES),
                       batch=b, seq=s, heads=diff_heads, q_blk0=dq0, k_blk0=dk0, vt_blk0=dvt0,
                       lambda_init=lambda_init)
    sb_g2 = jnp.tile(sb_norm_g[layer], 2).reshape(1, LANES)
    s_out = _sb_attn(proj, v_t, sb_g2, batch=b, seq=s, pairs=sb_pairs,
                     q_blk0=sq0, k_blk0=sk0, vt_blk0=svt0)
    out = _out_ffn(x2, d_out, s_out, w_out[layer].astype(BF16), norm_ffn_g[layer].reshape(1, d),
                   w_gate[layer].astype(BF16), w_up[layer].astype(BF16),
                   w_down[layer].astype(BF16), final_norm_g.reshape(1, d))
    return out.reshape(b, s, d)
```

```python
import functools
import math

import jax
import jax.numpy as jnp
from jax import lax
from jax.experimental import pallas as pl
from jax.experimental.pallas import tpu as pltpu

F32 = jnp.float32
BF16 = jnp.bfloat16

LANES = 128
MXU_DIM = 256
HEAD_DIM = 64
ROPE_DIM = HEAD_DIM // 4
ROPE_HALF = ROPE_DIM // 2
ROPE_THETA = 500000.0
NORM_EPS = 1e-5
LOG2E = math.log2(math.e)
Q_SCALE = HEAD_DIM ** -0.5 * LOG2E
SOFTPLUS_CLAMP = 64.0

VMEM_LIMIT_BYTES = 56 * 1024 * 1024

TOKEN_TILE = 512
ATTN_TILE = 256
DIFF_HEADS_PER_STEP = 2
SB_PAIRS_PER_STEP = 2
FFN_CHUNK = 256


def _rms_scale(x, eps=NORM_EPS):
    return lax.rsqrt(jnp.mean(x * x, axis=-1, keepdims=True) + eps)


def _dot(a, b):
    return jnp.dot(a, b, preferred_element_type=F32)


def _dot_nt(a, b):
    return lax.dot_general(a, b, (((1,), (1,)), ((), ())), preferred_element_type=F32)


def _lane_block(ref, rows, blk):
    return ref[rows, blk * LANES:(blk + 1) * LANES]


def _in_proj_kernel(x_ref, pos_ref, g_ref, invf_ref, w_ref, wvt_ref, o_ref, vt_ref, *, n_rope_blocks, q_blocks):
    x = x_ref[...]
    h = (x * _rms_scale(x) * g_ref[...]).astype(BF16)

    vt_ref[...] = _dot_nt(wvt_ref[...], h).astype(BF16)

    ang = invf_ref[...] * pos_ref[...]
    reps = LANES // ROPE_HALF
    cos = jnp.tile(jnp.cos(ang), (reps, 1)).T
    sin = jnp.tile(jnp.sin(ang), (reps, 1)).T
    lane = lax.broadcasted_iota(jnp.int32, (1, LANES), 1) % HEAD_DIM
    c_mul = jnp.where(lane < ROPE_DIM, cos, 1.0)
    s_below = jnp.where((lane >= ROPE_HALF) & (lane < ROPE_DIM), sin, 0.0)
    s_above = jnp.where(lane < ROPE_HALF, -sin, 0.0)

    per_dot = MXU_DIM // LANES
    for c in range(o_ref.shape[0]):
        t2 = _dot(h, w_ref[:, c * MXU_DIM:(c + 1) * MXU_DIM])
        for sub in range(per_dot):
            blk = c * per_dot + sub
            t = t2[:, sub * LANES:(sub + 1) * LANES]
            if blk < n_rope_blocks:
                t = (t * c_mul
                     + pltpu.roll(t, ROPE_HALF, 1) * s_below
                     + pltpu.roll(t, LANES - ROPE_HALF, 1) * s_above)
            if blk in q_blocks:
                t = t * Q_SCALE
            o_ref[c, :, sub * LANES:(sub + 1) * LANES] = t.astype(BF16)


def _in_proj(x2, pos2, g, invf, w_bf, wvt_bf, *, n_rope_blocks, q_blocks):
    n, d = x2.shape
    cols = w_bf.shape[1]
    vrows = wvt_bf.shape[0]
    tm = TOKEN_TILE
    kern = functools.partial(_in_proj_kernel, n_rope_blocks=n_rope_blocks, q_blocks=q_blocks)
    const = lambda i: (0, 0)
    return pl.pallas_call(
        kern,
        out_shape=(jax.ShapeDtypeStruct((cols // MXU_DIM, n, MXU_DIM), BF16), jax.ShapeDtypeStruct((vrows, n), BF16)),
        grid=(n // tm,),
        in_specs=[
            pl.BlockSpec((tm, d), lambda i: (i, 0)),
            pl.BlockSpec((None, 1, tm), lambda i: (i, 0, 0)),
            pl.BlockSpec((1, d), const),
            pl.BlockSpec((ROPE_HALF, 1), const),
            pl.BlockSpec((d, cols), const),
            pl.BlockSpec((vrows, d), const),
        ],
        out_specs=(pl.BlockSpec((cols // MXU_DIM, tm, MXU_DIM), lambda i: (0, i, 0)),
                   pl.BlockSpec((vrows, tm), lambda i: (0, i))),
        compiler_params=pltpu.CompilerParams(vmem_limit_bytes=VMEM_LIMIT_BYTES),
        name="in_proj",
    )(x2, pos2, g, invf, w_bf, wvt_bf)


def _diff_attn_kernel(lam_ref, g_ref, q_ref, k_ref, vt_ref, o_ref, s_a, s_b, m_ref, l_ref, acc_ref,
                      *, lambda_init, heads):
    t = ATTN_TILE
    nq = q_ref.shape[0] // t
    n = 2 * heads

    first = lax.broadcasted_iota(jnp.int32, (1, LANES), 1) < HEAD_DIM
    key = lax.broadcasted_iota(jnp.int32, (t, t), 0)
    qry = lax.broadcasted_iota(jnp.int32, (t, t), 1)

    def block_rows(blk):
        return pl.ds(pl.multiple_of(blk * t, t), t)

    def q_map(c, i):
        q = _lane_block(q_ref, block_rows(i), c // 2)
        zero = jnp.zeros_like(q)
        return jnp.where(first, q, zero) if c % 2 == 0 else jnp.where(first, zero, q)

    def scores_into(s_ref, c, i, blk):
        s_ref[c] = _dot_nt(_lane_block(k_ref, block_rows(blk), c // 2), q_map(c, i))

    def softmax_stage(s_ref, c, blk, causal):
        s = s_ref[c]
        if causal is not None:
            s = jnp.where(causal, s, -jnp.inf)
        m_old = m_ref[c]
        m_new = jnp.maximum(m_old, jnp.max(s, axis=0, keepdims=True))
        alpha = jnp.exp2(m_old - m_new)
        p = jnp.exp2(s - m_new)
        l_ref[c] = alpha * l_ref[c] + jnp.sum(p, axis=0, keepdims=True)
        m_ref[c] = m_new
        h = c // 2
        pv = _dot(vt_ref[h * LANES:(h + 1) * LANES, block_rows(blk)], p.astype(BF16))
        acc_ref[c] = alpha * acc_ref[c] + pv

    def step(s_cur, s_next, i, blk, causal, prefetch=True):
        if prefetch:
            nxt = jnp.maximum(blk - 1, 0)
            for c in range(n):
                scores_into(s_next, c, i, nxt)
        for c in range(n):
            softmax_stage(s_cur, c, blk, causal)

    lam_v = lam_ref[...]
    lam = (jnp.exp(jnp.sum(lam_v[0:1] * lam_v[1:2], axis=-1, keepdims=True))
           - jnp.exp(jnp.sum(lam_v[2:3] * lam_v[3:4], axis=-1, keepdims=True))
           + lambda_init)

    def query_tile(i, _):
        for c in range(n):
            m_ref[c] = jnp.full((1, t), -jnp.inf, F32)
            l_ref[c] = jnp.zeros((1, t), F32)
            acc_ref[c] = jnp.zeros((LANES, t), F32)
        step(s_a, s_b, i, i, key <= qry)

        def pair_body(p, _):
            blk = i - 1 - 2 * p
            step(s_b, s_a, i, blk, None)
            step(s_a, s_b, i, blk - 1, None)
            return 0

        lax.fori_loop(0, jnp.maximum(i - 1, 0) // 2, pair_body, 0)

        @pl.when(i % 2 == 1)
        def _():
            step(s_b, s_a, i, 0, None, prefetch=False)

        @pl.when((i % 2 == 0) & (i >= 2))
        def _():
            step(s_b, s_a, i, 1, None)
            step(s_a, s_b, i, 0, None, prefetch=False)

        nxt_tile = jnp.minimum(i + 1, nq - 1)
        for c in range(n):
            scores_into(s_a, c, nxt_tile, nxt_tile)
        for h in range(heads):
            o_t = acc_ref[2 * h] / l_ref[2 * h] - lam * (acc_ref[2 * h + 1] / l_ref[2 * h + 1])
            o_t = o_t * lax.rsqrt(jnp.mean(o_t * o_t, axis=0, keepdims=True) + NORM_EPS)
            o = o_t.T * g_ref[...] * (1.0 - lambda_init)
            o_ref[block_rows(i), h * LANES:(h + 1) * LANES] = o.astype(o_ref.dtype)
        return 0

    for c in range(n):
        scores_into(s_a, c, 0, 0)
    lax.fori_loop(0, nq, query_tile, 0)


def _diff_attn(proj, v_t, lam_vecs, g, *, batch, seq, heads, q_blk0, k_blk0, vt_blk0, lambda_init):
    t = ATTN_TILE
    hps = DIFF_HEADS_PER_STEP
    assert heads % hps == 0 and q_blk0 % hps == 0 and k_blk0 % hps == 0 and vt_blk0 % hps == 0
    assert proj.shape[2] == hps * LANES
    kern = functools.partial(_diff_attn_kernel, lambda_init=lambda_init, heads=hps)
    w = hps * LANES
    return pl.pallas_call(
        kern,
        out_shape=jax.ShapeDtypeStruct((heads // hps, batch * seq, w), BF16),
        grid=(batch, heads // hps),
        in_specs=[
            pl.BlockSpec((4, HEAD_DIM), lambda b, h: (0, 0)),
            pl.BlockSpec((1, LANES), lambda b, h: (0, 0)),
            pl.BlockSpec((None, seq, w), lambda b, h: (q_blk0 // hps + h, b, 0)),
            pl.BlockSpec((None, seq, w), lambda b, h: (k_blk0 // hps + h, b, 0)),
            pl.BlockSpec((w, seq), lambda b, h: (vt_blk0 // hps + h, b)),
        ],
        out_specs=pl.BlockSpec((None, seq, w), lambda b, h: (h, b, 0)),
        scratch_shapes=[
            pltpu.VMEM((2 * hps, t, t), F32), pltpu.VMEM((2 * hps, t, t), F32),
            pltpu.VMEM((2 * hps, 1, t), F32), pltpu.VMEM((2 * hps, 1, t), F32),
            pltpu.VMEM((2 * hps, LANES, t), F32),
        ],
        compiler_params=pltpu.CompilerParams(vmem_limit_bytes=VMEM_LIMIT_BYTES),
        name="diff_attn",
    )(lam_vecs, g, proj, proj, v_t)


def _sb_attn_kernel(g_ref, q_ref, k_ref, vt_ref, o_ref, ws_a, ws_b, cy_a, cy_b, a_buf, acc_ref, sc_ref, *, pairs):
    t = ATTN_TILE
    nq = q_ref.shape[0] // t
    n = 2 * pairs

    first = lax.broadcasted_iota(jnp.int32, (1, LANES), 1) < HEAD_DIM
    key = lax.broadcasted_iota(jnp.int32, (t, t), 0)
    qry = lax.broadcasted_iota(jnp.int32, (t, t), 1)
    tri = (qry >= key).astype(BF16)

    def block_rows(blk):
        return pl.ds(pl.multiple_of(blk * t, t), t)

    def q_head(c, i):
        q = _lane_block(q_ref, block_rows(i), c // 2)
        zero = jnp.zeros_like(q)
        return jnp.where(first, q, zero) if c % 2 == 0 else jnp.where(first, zero, q)

    def logits_into(ws, c, i, blk):
        ws[c] = _dot_nt(_lane_block(k_ref, block_rows(blk), c // 2), q_head(c, i))

    def values_into_acc(c, blk):
        v_t = vt_ref[c * HEAD_DIM:(c + 1) * HEAD_DIM, block_rows(blk)]
        acc_ref[c] += sc_ref[c] * _dot(v_t, a_buf[c])

    def cumsum_stage(ws, cy_in, cy_out, c, strict):
        w = ws[c]
        sp = jnp.log(1.0 + jnp.exp2(jnp.minimum(w, SOFTPLUS_CLAMP))) * LOG2E
        neg_l = jnp.maximum(w, sp)
        if strict is not None:
            neg_l = jnp.where(strict, neg_l, 0.0)
        incl = _dot(tri, neg_l.astype(BF16))
        cy_out[c] = cy_in[c] + incl[0:1, :]
        return incl

    def weight_stage(ws, cy_in, c, incl, strict):
        a = jnp.exp2(ws[c] - incl)
        if strict is not None:
            a = jnp.where(strict, a, 0.0)
        a_buf[c] = a.astype(BF16)
        sc_ref[c] = jnp.exp2(-cy_in[c])

    def step(ws, ws_next, cy_in, cy_out, i, blk, strict, prev_values, prefetch=True):
        nxt = jnp.maximum(blk - 1, 0)
        incl = [None] * n
        for c in range(n):
            if prefetch and c == 0:
                logits_into(ws_next, 0, i, nxt)
            if prefetch and c + 1 < n:
                logits_into(ws_next, c + 1, i, nxt)
            incl[c] = cumsum_stage(ws, cy_in, cy_out, c, strict)
            if prev_values:
                values_into_acc(c, blk + 1)
        for c in range(n):
            weight_stage(ws, cy_in, c, incl[c], strict)

    def query_tile(i, _):
        for c in range(n):
            cy_a[c] = jnp.zeros((1, t), F32)
            acc_ref[c] = jnp.zeros((HEAD_DIM, t), F32)
        step(ws_a, ws_b, cy_a, cy_b, i, i, key < qry, False)

        def pair_body(p, _):
            blk = i - 1 - 2 * p
            step(ws_b, ws_a, cy_b, cy_a, i, blk, None, True)
            step(ws_a, ws_b, cy_a, cy_b, i, blk - 1, None, True)
            return 0

        lax.fori_loop(0, jnp.maximum(i - 1, 0) // 2, pair_body, 0)

        @pl.when(i % 2 == 1)
        def _():
            step(ws_b, ws_a, cy_b, cy_a, i, 0, None, True, prefetch=False)

        @pl.when((i % 2 == 0) & (i >= 2))
        def _():
            step(ws_b, ws_a, cy_b, cy_a, i, 1, None, True)
            step(ws_a, ws_b, cy_a, cy_b, i, 0, None, True, prefetch=False)

        for c in range(n):
            values_into_acc(c, 0)
        nxt_tile = jnp.minimum(i + 1, nq - 1)
        for c in range(n):
            logits_into(ws_a, c, nxt_tile, nxt_tile)
        for p in range(pairs):
            halves = []
            for c in (2 * p, 2 * p + 1):
                o_t = acc_ref[c]
                halves.append(o_t * lax.rsqrt(jnp.mean(o_t * o_t, axis=0, keepdims=True) + NORM_EPS))
            o = jnp.concatenate(halves, axis=0).T * g_ref[...]
            o_ref[block_rows(i), p * LANES:(p + 1) * LANES] = o.astype(o_ref.dtype)
        return 0

    for c in range(n):
        logits_into(ws_a, c, 0, 0)
    lax.fori_loop(0, nq, query_tile, 0)


def _sb_attn(proj, v_t, g2, *, batch, seq, pairs, q_blk0, k_blk0, vt_blk0):
    t = ATTN_TILE
    pps = SB_PAIRS_PER_STEP
    assert pairs % pps == 0 and q_blk0 % pps == 0 and k_blk0 % pps == 0 and vt_blk0 % pps == 0
    assert proj.shape[2] == pps * LANES
    w = pps * LANES
    return pl.pallas_call(
        functools.partial(_sb_attn_kernel, pairs=pps),
        out_shape=jax.ShapeDtypeStruct((pairs // pps, batch * seq, w), BF16),
        grid=(batch, pairs // pps),
        in_specs=[
            pl.BlockSpec((1, LANES), lambda b, h: (0, 0)),
            pl.BlockSpec((None, seq, w), lambda b, h: (q_blk0 // pps + h, b, 0)),
            pl.BlockSpec((None, seq, w), lambda b, h: (k_blk0 // pps + h, b, 0)),
            pl.BlockSpec((w, seq), lambda b, h: (vt_blk0 // pps + h, b)),
        ],
        out_specs=pl.BlockSpec((None, seq, w), lambda b, h: (h, b, 0)),
        scratch_shapes=[
            pltpu.VMEM((2 * pps, t, t), F32), pltpu.VMEM((2 * pps, t, t), F32),
            pltpu.VMEM((2 * pps, 1, t), F32), pltpu.VMEM((2 * pps, 1, t), F32),
            pltpu.VMEM((2 * pps, t, t), BF16),
            pltpu.VMEM((2 * pps, HEAD_DIM, t), F32),
            pltpu.VMEM((2 * pps, 1, t), F32),
        ],
        compiler_params=pltpu.CompilerParams(vmem_limit_bytes=VMEM_LIMIT_BYTES),
        name="sb_attn",
    )(g2, proj, proj, v_t)


def _out_ffn_kernel(x_ref, d_ref, s_ref, wo_ref, g2_ref, wg_ref, wu_ref, wd_ref, gf_ref,
                    o_ref, act_ref):
    x1 = x_ref[...]
    row = 0
    for mix_ref in (d_ref, s_ref):
        for grp in range(mix_ref.shape[0]):
            gw = mix_ref.shape[2]
            x1 = x1 + _dot(mix_ref[grp], wo_ref[row:row + gw, :])
            row += gw
    h2 = (x1 * _rms_scale(x1) * g2_ref[...]).astype(BF16)
    d_ff = wg_ref.shape[1]
    for c in range(d_ff // FFN_CHUNK):
        cols = slice(c * FFN_CHUNK, (c + 1) * FFN_CHUNK)
        gate = _dot(h2, wg_ref[:, cols])
        up = _dot(h2, wu_ref[:, cols])
        act_ref[:, cols] = (gate * jax.nn.sigmoid(gate) * up).astype(BF16)
    y = x1 + _dot(act_ref[...], wd_ref[...])
    o_ref[...] = y * _rms_scale(y) * gf_ref[...]


def _out_ffn(x2, d_out, s_out, wo, g2, wg, wu, wd, gf):
    n, d = x2.shape
    d_ff = wg.shape[1]
    tm = TOKEN_TILE
    assert d_ff % FFN_CHUNK == 0
    const = lambda i: (0, 0)
    resident = functools.partial(pl.BlockSpec, index_map=const, pipeline_mode=pl.Buffered(1))
    return pl.pallas_call(
        _out_ffn_kernel,
        out_shape=jax.ShapeDtypeStruct((n, d), F32),
        grid=(n // tm,),
        in_specs=[
            pl.BlockSpec((tm, d), lambda i: (i, 0)),
            pl.BlockSpec((d_out.shape[0], tm, d_out.shape[2]), lambda i: (0, i, 0)),
            pl.BlockSpec((s_out.shape[0], tm, s_out.shape[2]), lambda i: (0, i, 0)),
            resident(wo.shape),
            pl.BlockSpec((1, d), const),
            resident(wg.shape),
            resident(wu.shape),
            resident(wd.shape),
            pl.BlockSpec((1, d), const),
        ],
        out_specs=pl.BlockSpec((tm, d), lambda i: (i, 0)),
        scratch_shapes=[pltpu.VMEM((tm, d_ff), BF16)],
        compiler_params=pltpu.CompilerParams(vmem_limit_bytes=VMEM_LIMIT_BYTES),
        name="out_ffn",
    )(x2, d_out, s_out, wo, g2, wg, wu, wd, gf)


def kernel(x, positions, norm_attn_g, w_in, lambda_q1, lambda_k1, lambda_q2, lambda_k2,
           diff_subln_g, sb_norm_g, w_out, norm_ffn_g, w_gate, w_up, w_down, final_norm_g):
    b, s, d = x.shape
    depth = w_in.shape[0]
    assert depth == 1
    diff_heads = d // (2 * 2 * HEAD_DIM)
    sb_pairs = d // (2 * HEAD_DIM) // 2
    assert diff_subln_g.shape[1] == LANES and sb_norm_g.shape[1] == HEAD_DIM
    assert w_in.shape[2] == 3 * diff_heads * LANES + 3 * sb_pairs * LANES
    dq0, dk0 = 0, diff_heads
    sq0 = 2 * diff_heads
    sk0 = sq0 + sb_pairs
    q_blocks = frozenset(range(dq0, dk0)) | frozenset(range(sq0, sk0))
    dvt0, svt0 = 0, diff_heads

    x2 = x.reshape(b * s, d)
    pos2 = positions.astype(F32).reshape(b * s // TOKEN_TILE, 1, TOKEN_TILE)
    inv_freq = ROPE_THETA ** (-jnp.arange(ROPE_HALF, dtype=F32) / ROPE_HALF)
    invf = inv_freq.reshape(ROPE_HALF, 1)

    layer = 0
    lambda_init = 0.8 - 0.6 * math.exp(-0.3 * layer)
    qk_w = 2 * diff_heads * LANES
    v_w = diff_heads * LANES
    sb_w = sb_pairs * LANES
    w = w_in[layer].astype(BF16)
    w_rows = jnp.concatenate([w[:, :qk_w], w[:, qk_w + v_w:qk_w + v_w + 2 * sb_w]], axis=1)
    w_vt = jnp.concatenate([w[:, qk_w:qk_w + v_w], w[:, qk_w + v_w + 2 * sb_w:]], axis=1).T
    proj, v_t = _in_proj(x2, pos2, norm_attn_g[layer].reshape(1, d), invf, w_rows, w_vt,
                         n_rope_blocks=sq0, q_blocks=q_blocks)
    lam_vecs = jnp.stack([lambda_q1[layer], lambda_k1[layer], lambda_q2[layer], lambda_k2[layer]])
    d_out = _diff_attn(proj, v_t, lam_vecs.astype(F32), diff_subln_g[layer].reshape(1, LANES),
                       batch=b, seq=s, heads=diff_heads, q_blk0=dq0, k_blk0=dk0, vt_blk0=dvt0,
                       lambda_init=lambda_init)
    sb_g2 = jnp.tile(sb_norm_g[layer], 2).reshape(1, LANES)
    s_out = _sb_attn(proj, v_t, sb_g2, batch=b, seq=s, pairs=sb_pairs,
                     q_blk0=sq0, k_blk0=sk0, vt_blk0=svt0)
    out = _out_ffn(x2, d_out, s_out, w_out[layer].astype(BF16), norm_ffn_g[layer].reshape(1, d),
                   w_gate[layer].astype(BF16), w_up[layer].astype(BF16),
                   w_down[layer].astype(BF16), final_norm_g.reshape(1, d))
    return out.reshape(b, s, d)
```

```python
import functools
import math

import jax
import jax.numpy as jnp
from jax import lax
from jax.experimental import pallas as pl
from jax.experimental.pallas import tpu as pltpu

F32 = jnp.float32
BF16 = jnp.bfloat16

LANES = 128
MXU_DIM = 256
HEAD_DIM = 64
ROPE_DIM = HEAD_DIM // 4
ROPE_HALF = ROPE_DIM // 2
ROPE_THETA = 500000.0
NORM_EPS = 1e-5
LOG2E = math.log2(math.e)
Q_SCALE = HEAD_DIM ** -0.5 * LOG2E
SOFTPLUS_CLAMP = 64.0

VMEM_LIMIT_BYTES = 56 * 1024 * 1024

TOKEN_TILE = 512
ATTN_TILE = 512
DIFF_HEADS_PER_STEP = 2
SB_PAIRS_PER_STEP = 2
FFN_CHUNK = 256


def _rms_scale(x, eps=NORM_EPS):
    return lax.rsqrt(jnp.mean(x * x, axis=-1, keepdims=True) + eps)


def _dot(a, b):
    return jnp.dot(a, b, preferred_element_type=F32)


def _dot_nt(a, b):
    return lax.dot_general(a, b, (((1,), (1,)), ((), ())), preferred_element_type=F32)


def _lane_block(ref, rows, blk):
    return ref[rows, blk * LANES:(blk + 1) * LANES]


def _in_proj_kernel(x_ref, pos_ref, g_ref, invf_ref, w_ref, wvt_ref, o_ref, vt_ref, *, n_rope_blocks, q_blocks):
    x = x_ref[...]
    h = (x * _rms_scale(x) * g_ref[...]).astype(BF16)

    vt_ref[...] = _dot_nt(wvt_ref[...], h).astype(BF16)

    ang = invf_ref[...] * pos_ref[...]
    reps = LANES // ROPE_HALF
    cos = jnp.tile(jnp.cos(ang), (reps, 1)).T
    sin = jnp.tile(jnp.sin(ang), (reps, 1)).T
    lane = lax.broadcasted_iota(jnp.int32, (1, LANES), 1) % HEAD_DIM
    c_mul = jnp.where(lane < ROPE_DIM, cos, 1.0)
    s_below = jnp.where((lane >= ROPE_HALF) & (lane < ROPE_DIM), sin, 0.0)
    s_above = jnp.where(lane < ROPE_HALF, -sin, 0.0)

    per_dot = MXU_DIM // LANES
    for c in range(o_ref.shape[0]):
        t2 = _dot(h, w_ref[:, c * MXU_DIM:(c + 1) * MXU_DIM])
        for sub in range(per_dot):
            blk = c * per_dot + sub
            t = t2[:, sub * LANES:(sub + 1) * LANES]
            if blk < n_rope_blocks:
                t = (t * c_mul
                     + pltpu.roll(t, ROPE_HALF, 1) * s_below
                     + pltpu.roll(t, LANES - ROPE_HALF, 1) * s_above)
            if blk in q_blocks:
                t = t * Q_SCALE
            o_ref[c, :, sub * LANES:(sub + 1) * LANES] = t.astype(BF16)


def _in_proj(x2, pos2, g, invf, w_bf, wvt_bf, *, n_rope_blocks, q_blocks):
    n, d = x2.shape
    cols = w_bf.shape[1]
    vrows = wvt_bf.shape[0]
    tm = TOKEN_TILE
    kern = functools.partial(_in_proj_kernel, n_rope_blocks=n_rope_blocks, q_blocks=q_blocks)
    const = lambda i: (0, 0)
    return pl.pallas_call(
        kern,
        out_shape=(jax.ShapeDtypeStruct((cols // MXU_DIM, n, MXU_DIM), BF16), jax.ShapeDtypeStruct((vrows, n), BF16)),
        grid=(n // tm,),
        in_specs=[
            pl.BlockSpec((tm, d), lambda i: (i, 0)),
            pl.BlockSpec((None, 1, tm), lambda i: (i, 0, 0)),
            pl.BlockSpec((1, d), const),
            pl.BlockSpec((ROPE_HALF, 1), const),
            pl.BlockSpec((d, cols), const),
            pl.BlockSpec((vrows, d), const),
        ],
        out_specs=(pl.BlockSpec((cols // MXU_DIM, tm, MXU_DIM), lambda i: (0, i, 0)),
                   pl.BlockSpec((vrows, tm), lambda i: (0, i))),
        compiler_params=pltpu.CompilerParams(vmem_limit_bytes=VMEM_LIMIT_BYTES),
        name="in_proj",
    )(x2, pos2, g, invf, w_bf, wvt_bf)


def _diff_attn_kernel(lam_ref, g_ref, q_ref, k_ref, vt_ref, o_ref, s_a, s_b, m_ref, l_ref, acc_ref,
                      *, lambda_init, heads):
    t = ATTN_TILE
    nq = q_ref.shape[0] // t
    n = 2 * heads

    first = lax.broadcasted_iota(jnp.int32, (1, LANES), 1) < HEAD_DIM
    key = lax.broadcasted_iota(jnp.int32, (t, t), 0)
    qry = lax.broadcasted_iota(jnp.int32, (t, t), 1)

    def block_rows(blk):
        return pl.ds(pl.multiple_of(blk * t, t), t)

    def q_map(c, i):
        q = _lane_block(q_ref, block_rows(i), c // 2)
        zero = jnp.zeros_like(q)
        return jnp.where(first, q, zero) if c % 2 == 0 else jnp.where(first, zero, q)

    def scores_into(s_ref, c, i, blk):
        s_ref[c] = _dot_nt(_lane_block(k_ref, block_rows(blk), c // 2), q_map(c, i))

    def softmax_stage(s_ref, c, blk, causal):
        s = s_ref[c]
        if causal is not None:
            s = jnp.where(causal, s, -jnp.inf)
        m_old = m_ref[c]
        m_new = jnp.maximum(m_old, jnp.max(s, axis=0, keepdims=True))
        alpha = jnp.exp2(m_old - m_new)
        p = jnp.exp2(s - m_new)
        l_ref[c] = alpha * l_ref[c] + jnp.sum(p, axis=0, keepdims=True)
        m_ref[c] = m_new
        h = c // 2
        pv = _dot(vt_ref[h * LANES:(h + 1) * LANES, block_rows(blk)], p.astype(BF16))
        acc_ref[c] = alpha * acc_ref[c] + pv

    def step(s_cur, s_next, i, blk, causal, prefetch=True):
        if prefetch:
            nxt = jnp.maximum(blk - 1, 0)
            for c in range(n):
                scores_into(s_next, c, i, nxt)
        for c in range(n):
            softmax_stage(s_cur, c, blk, causal)

    lam_v = lam_ref[...]
    lam = (jnp.exp(jnp.sum(lam_v[0:1] * lam_v[1:2], axis=-1, keepdims=True))
           - jnp.exp(jnp.sum(lam_v[2:3] * lam_v[3:4], axis=-1, keepdims=True))
           + lambda_init)

    def query_tile(i, _):
        for c in range(n):
            m_ref[c] = jnp.full((1, t), -jnp.inf, F32)
            l_ref[c] = jnp.zeros((1, t), F32)
            acc_ref[c] = jnp.zeros((LANES, t), F32)
        step(s_a, s_b, i, i, key <= qry)

        def pair_body(p, _):
            blk = i - 1 - 2 * p
            step(s_b, s_a, i, blk, None)
            step(s_a, s_b, i, blk - 1, None)
            return 0

        lax.fori_loop(0, jnp.maximum(i - 1, 0) // 2, pair_body, 0)

        @pl.when(i % 2 == 1)
        def _():
            step(s_b, s_a, i, 0, None, prefetch=False)

        @pl.when((i % 2 == 0) & (i >= 2))
        def _():
            step(s_b, s_a, i, 1, None)
            step(s_a, s_b, i, 0, None, prefetch=False)

        nxt_tile = jnp.minimum(i + 1, nq - 1)
        for c in range(n):
            scores_into(s_a, c, nxt_tile, nxt_tile)
        for h in range(heads):
            o_t = acc_ref[2 * h] / l_ref[2 * h] - lam * (acc_ref[2 * h + 1] / l_ref[2 * h + 1])
            o_t = o_t * lax.rsqrt(jnp.mean(o_t * o_t, axis=0, keepdims=True) + NORM_EPS)
            o = o_t.T * g_ref[...] * (1.0 - lambda_init)
            o_ref[block_rows(i), h * LANES:(h + 1) * LANES] = o.astype(o_ref.dtype)
        return 0

    for c in range(n):
        scores_into(s_a, c, 0, 0)
    lax.fori_loop(0, nq, query_tile, 0)


def _diff_attn(proj, v_t, lam_vecs, g, *, batch, seq, heads, q_blk0, k_blk0, vt_blk0, lambda_init):
    t = ATTN_TILE
    hps = DIFF_HEADS_PER_STEP
    assert heads % hps == 0 and q_blk0 % hps == 0 and k_blk0 % hps == 0 and vt_blk0 % hps == 0
    assert proj.shape[2] == hps * LANES
    kern = functools.partial(_diff_attn_kernel, lambda_init=lambda_init, heads=hps)
    w = hps * LANES
    return pl.pallas_call(
        kern,
        out_shape=jax.ShapeDtypeStruct((heads // hps, batch * seq, w), BF16),
        grid=(batch, heads // hps),
        in_specs=[
            pl.BlockSpec((4, HEAD_DIM), lambda b, h: (0, 0)),
            pl.BlockSpec((1, LANES), lambda b, h: (0, 0)),
            pl.BlockSpec((None, seq, w), lambda b, h: (q_blk0 // hps + h, b, 0)),
            pl.BlockSpec((None, seq, w), lambda b, h: (k_blk0 // hps + h, b, 0)),
            pl.BlockSpec((w, seq), lambda b, h: (vt_blk0 // hps + h, b)),
        ],
        out_specs=pl.BlockSpec((None, seq, w), lambda b, h: (h, b, 0)),
        scratch_shapes=[
            pltpu.VMEM((2 * hps, t, t), F32), pltpu.VMEM((2 * hps, t, t), F32),
            pltpu.VMEM((2 * hps, 1, t), F32), pltpu.VMEM((2 * hps, 1, t), F32),
            pltpu.VMEM((2 * hps, LANES, t), F32),
        ],
        compiler_params=pltpu.CompilerParams(vmem_limit_bytes=VMEM_LIMIT_BYTES),
        name="diff_attn",
    )(lam_vecs, g, proj, proj, v_t)


def _sb_attn_kernel(g_ref, q_ref, k_ref, vt_ref, o_ref, ws_a, ws_b, cy_a, cy_b, a_buf, acc_ref, sc_ref, *, pairs):
    t = ATTN_TILE
    nq = q_ref.shape[0] // t
    n = 2 * pairs

    first = lax.broadcasted_iota(jnp.int32, (1, LANES), 1) < HEAD_DIM
    key = lax.broadcasted_iota(jnp.int32, (t, t), 0)
    qry = lax.broadcasted_iota(jnp.int32, (t, t), 1)
    tri = (qry >= key).astype(BF16)

    def block_rows(blk):
        return pl.ds(pl.multiple_of(blk * t, t), t)

    def q_head(c, i):
        q = _lane_block(q_ref, block_rows(i), c // 2)
        zero = jnp.zeros_like(q)
        return jnp.where(first, q, zero) if c % 2 == 0 else jnp.where(first, zero, q)

    def logits_into(ws, c, i, blk):
        ws[c] = _dot_nt(_lane_block(k_ref, block_rows(blk), c // 2), q_head(c, i))

    def values_into_acc(c, blk):
        v_t = vt_ref[c * HEAD_DIM:(c + 1) * HEAD_DIM, block_rows(blk)]
        acc_ref[c] += sc_ref[c] * _dot(v_t, a_buf[c])

    def cumsum_stage(ws, cy_in, cy_out, c, strict):
        w = ws[c]
        sp = jnp.log(1.0 + jnp.exp2(jnp.minimum(w, SOFTPLUS_CLAMP))) * LOG2E
        neg_l = jnp.maximum(w, sp)
        if strict is not None:
            neg_l = jnp.where(strict, neg_l, 0.0)
        incl = _dot(tri, neg_l.astype(BF16))
        cy_out[c] = cy_in[c] + incl[0:1, :]
        return incl

    def weight_stage(ws, cy_in, c, incl, strict):
        a = jnp.exp2(ws[c] - incl)
        if strict is not None:
            a = jnp.where(strict, a, 0.0)
        a_buf[c] = a.astype(BF16)
        sc_ref[c] = jnp.exp2(-cy_in[c])

    def step(ws, ws_next, cy_in, cy_out, i, blk, strict, prev_values, prefetch=True):
        nxt = jnp.maximum(blk - 1, 0)
        incl = [None] * n
        for c in range(n):
            if prefetch and c == 0:
                logits_into(ws_next, 0, i, nxt)
            if prefetch and c + 1 < n:
                logits_into(ws_next, c + 1, i, nxt)
            incl[c] = cumsum_stage(ws, cy_in, cy_out, c, strict)
            if prev_values:
                values_into_acc(c, blk + 1)
        for c in range(n):
            weight_stage(ws, cy_in, c, incl[c], strict)

    def query_tile(i, _):
        for c in range(n):
            cy_a[c] = jnp.zeros((1, t), F32)
            acc_ref[c] = jnp.zeros((HEAD_DIM, t), F32)
        step(ws_a, ws_b, cy_a, cy_b, i, i, key < qry, False)

        def pair_body(p, _):
            blk = i - 1 - 2 * p
            step(ws_b, ws_a, cy_b, cy_a, i, blk, None, True)
            step(ws_a, ws_b, cy_a, cy_b, i, blk - 1, None, True)
            return 0

        lax.fori_loop(0, jnp.maximum(i - 1, 0) // 2, pair_body, 0)

        @pl.when(i % 2 == 1)
        def _():
            step(ws_b, ws_a, cy_b, cy_a, i, 0, None, True, prefetch=False)

        @pl.when((i % 2 == 0) & (i >= 2))
        def _():
            step(ws_b, ws_a, cy_b, cy_a, i, 1, None, True)
            step(ws_a, ws_b, cy_a, cy_b, i, 0, None, True, prefetch=False)

        for c in range(n):
            values_into_acc(c, 0)
        nxt_tile = jnp.minimum(i + 1, nq - 1)
        for c in range(n):
            logits_into(ws_a, c, nxt_tile, nxt_tile)
        for p in range(pairs):
            halves = []
            for c in (2 * p, 2 * p + 1):
                o_t = acc_ref[c]
                halves.append(o_t * lax.rsqrt(jnp.mean(o_t * o_t, axis=0, keepdims=True) + NORM_EPS))
            o = jnp.concatenate(halves, axis=0).T * g_ref[...]
            o_ref[block_rows(i), p * LANES:(p + 1) * LANES] = o.astype(o_ref.dtype)
        return 0

    for c in range(n):
        logits_into(ws_a, c, 0, 0)
    lax.fori_loop(0, nq, query_tile, 0)


def _sb_attn(proj, v_t, g2, *, batch, seq, pairs, q_blk0, k_blk0, vt_blk0):
    t = ATTN_TILE
    pps = SB_PAIRS_PER_STEP
    assert pairs % pps == 0 and q_blk0 % pps == 0 and k_blk0 % pps == 0 and vt_blk0 % pps == 0
    assert proj.shape[2] == pps * LANES
    w = pps * LANES
    return pl.pallas_call(
        functools.partial(_sb_attn_kernel, pairs=pps),
        out_shape=jax.ShapeDtypeStruct((pairs // pps, batch * seq, w), BF16),
        grid=(batch, pairs // pps),
        in_specs=[
            pl.BlockSpec((1, LANES), lambda b, h: (0, 0)),
            pl.BlockSpec((None, seq, w), lambda b, h: (q_blk0 // pps + h, b, 0)),
            pl.BlockSpec((None, seq, w), lambda b, h: (k_blk0 // pps + h, b, 0)),
            pl.BlockSpec((w, seq), lambda b, h: (vt_blk0 // pps + h, b)),
        ],
        out_specs=pl.BlockSpec((None, seq, w), lambda b, h: (h, b, 0)),
        scratch_shapes=[
            pltpu.VMEM((2 * pps, t, t), F32), pltpu.VMEM((2 * pps, t, t), F32),
            pltpu.VMEM((2 * pps, 1, t), F32), pltpu.VMEM((2 * pps, 1, t), F32),
            pltpu.VMEM((2 * pps, t, t), BF16),
            pltpu.VMEM((2 * pps, HEAD_DIM, t), F32),
            pltpu.VMEM((2 * pps, 1, t), F32),
        ],
        compiler_params=pltpu.CompilerParams(vmem_limit_bytes=VMEM_LIMIT_BYTES),
        name="sb_attn",
    )(g2, proj, proj, v_t)


def _out_ffn_kernel(x_ref, d_ref, s_ref, wo_ref, g2_ref, wg_ref, wu_ref, wd_ref, gf_ref,
                    o_ref, act_ref):
    x1 = x_ref[...]
    row = 0
    for mix_ref in (d_ref, s_ref):
        for grp in range(mix_ref.shape[0]):
            gw = mix_ref.shape[2]
            x1 = x1 + _dot(mix_ref[grp], wo_ref[row:row + gw, :])
            row += gw
    h2 = (x1 * _rms_scale(x1) * g2_ref[...]).astype(BF16)
    d_ff = wg_ref.shape[1]
    for c in range(d_ff // FFN_CHUNK):
        cols = slice(c * FFN_CHUNK, (c + 1) * FFN_CHUNK)
        gate = _dot(h2, wg_ref[:, cols])
        up = _dot(h2, wu_ref[:, cols])
        act_ref[:, cols] = (gate * jax.nn.sigmoid(gate) * up).astype(BF16)
    y = x1 + _dot(act_ref[...], wd_ref[...])
    o_ref[...] = y * _rms_scale(y) * gf_ref[...]


def _out_ffn(x2, d_out, s_out, wo, g2, wg, wu, wd, gf):
    n, d = x2.shape
    d_ff = wg.shape[1]
    tm = TOKEN_TILE
    assert d_ff % FFN_CHUNK == 0
    const = lambda i: (0, 0)
    resident = functools.partial(pl.BlockSpec, index_map=const, pipeline_mode=pl.Buffered(1))
    return pl.pallas_call(
        _out_ffn_kernel,
        out_shape=jax.ShapeDtypeStruct((n, d), F32),
        grid=(n // tm,),
        in_specs=[
            pl.BlockSpec((tm, d), lambda i: (i, 0)),
            pl.BlockSpec((d_out.shape[0], tm, d_out.shape[2]), lambda i: (0, i, 0)),
            pl.BlockSpec((s_out.shape[0], tm, s_out.shape[2]), lambda i: (0, i, 0)),
            resident(wo.shape),
            pl.BlockSpec((1, d), const),
            resident(wg.shape),
            resident(wu.shape),
            resident(wd.shape),
            pl.BlockSpec((1, d), const),
        ],
        out_specs=pl.BlockSpec((tm, d), lambda i: (i, 0)),
        scratch_shapes=[pltpu.VMEM((tm, d_ff), BF16)],
        compiler_params=pltpu.CompilerParams(vmem_limit_bytes=VMEM_LIMIT_BYTES),
        name="out_ffn",
    )(x2, d_out, s_out, wo, g2, wg, wu, wd, gf)


def kernel(x, positions, norm_attn_g, w_in, lambda_q1, lambda_k1, lambda_q2, lambda_k2,
           diff_subln_g, sb_norm_g, w_out, norm_ffn_g, w_gate, w_up, w_down, final_norm_g):
    b, s, d = x.shape
    depth = w_in.shape[0]
    assert depth == 1
    diff_heads = d // (2 * 2 * HEAD_DIM)
    sb_pairs = d // (2 * HEAD_DIM) // 2
    assert diff_subln_g.shape[1] == LANES and sb_norm_g.shape[1] == HEAD_DIM
    assert w_in.shape[2] == 3 * diff_heads * LANES + 3 * sb_pairs * LANES
    dq0, dk0 = 0, diff_heads
    sq0 = 2 * diff_heads
    sk0 = sq0 + sb_pairs
    q_blocks = frozenset(range(dq0, dk0)) | frozenset(range(sq0, sk0))
    dvt0, svt0 = 0, diff_heads

    x2 = x.reshape(b * s, d)
    pos2 = positions.astype(F32).reshape(b * s // TOKEN_TILE, 1, TOKEN_TILE)
    inv_freq = ROPE_THETA ** (-jnp.arange(ROPE_HALF, dtype=F32) / ROPE_HALF)
    invf = inv_freq.reshape(ROPE_HALF, 1)

    layer = 0
    lambda_init = 0.8 - 0.6 * math.exp(-0.3 * layer)
    qk_w = 2 * diff_heads * LANES
    v_w = diff_heads * LANES
    sb_w = sb_pairs * LANES
    w = w_in[layer].astype(BF16)
    w_rows = jnp.concatenate([w[:, :qk_w], w[:, qk_w + v_w:qk_w + v_w + 2 * sb_w]], axis=1)
    w_vt = jnp.concatenate([w[:, qk_w:qk_w + v_w], w[:, qk_w + v_w + 2 * sb_w:]], axis=1).T
    proj, v_t = _in_proj(x2, pos2, norm_attn_g[layer].reshape(1, d), invf, w_rows, w_vt,
                         n_rope_blocks=sq0, q_blocks=q_blocks)
    lam_vecs = jnp.stack([lambda_q1[layer], lambda_k1[layer], lambda_q2[layer], lambda_k2[layer]])
    d_out = _diff_attn(proj, v_t, lam_vecs.astype(F32), diff_subln_g[layer].reshape(1, LANES),
                       batch=b, seq=s, heads=diff_heads, q_blk0=dq0, k_blk0=dk0, vt_blk0=dvt0,
                       lambda_init=lambda_init)
    sb_g2 = jnp.tile(sb_norm_g[layer], 2).reshape(1, LANES)
    s_out = _sb_attn(proj, v_t, sb_g2, batch=b, seq=s, pairs=sb_pairs,
                     q_blk0=sq0, k_blk0=sk0, vt_blk0=svt0)
    out = _out_ffn(x2, d_out, s_out, w_out[layer].astype(BF16), norm_ffn_g[layer].reshape(1, d),
                   w_gate[layer].astype(BF16), w_up[layer].astype(BF16),
                   w_down[layer].astype(BF16), final_norm_g.reshape(1, d))
    return out.reshape(b, s, d)
```

```python
import functools
import math

import jax
import jax.numpy as jnp
from jax import lax
from jax.experimental import pallas as pl
from jax.experimental.pallas import tpu as pltpu

F32 = jnp.float32
BF16 = jnp.bfloat16

LANES = 128
MXU_DIM = 256
HEAD_DIM = 64
ROPE_DIM = HEAD_DIM // 4
ROPE_HALF = ROPE_DIM // 2
ROPE_THETA = 500000.0
NORM_EPS = 1e-5
LOG2E = math.log2(math.e)
Q_SCALE = HEAD_DIM ** -0.5 * LOG2E
SOFTPLUS_CLAMP = 64.0

VMEM_LIMIT_BYTES = 56 * 1024 * 1024

TOKEN_TILE = 512
DIFF_TILE = 512
SB_TILE = 256
DIFF_HEADS_PER_STEP = 2
SB_PAIRS_PER_STEP = 2
FFN_CHUNK = 256


def _rms_scale(x, eps=NORM_EPS):
    return lax.rsqrt(jnp.mean(x * x, axis=-1, keepdims=True) + eps)


def _dot(a, b):
    return jnp.dot(a, b, preferred_element_type=F32)


def _dot_nt(a, b):
    return lax.dot_general(a, b, (((1,), (1,)), ((), ())), preferred_element_type=F32)


def _lane_block(ref, rows, blk):
    return ref[rows, blk * LANES:(blk + 1) * LANES]


def _in_proj_kernel(x_ref, pos_ref, g_ref, invf_ref, w_ref, wvt_ref, o_ref, vt_ref, *, n_rope_blocks, q_blocks):
    x = x_ref[...]
    h = (x * _rms_scale(x) * g_ref[...]).astype(BF16)

    vt_ref[...] = _dot_nt(wvt_ref[...], h).astype(BF16)

    ang = invf_ref[...] * pos_ref[...]
    reps = LANES // ROPE_HALF
    cos = jnp.tile(jnp.cos(ang), (reps, 1)).T
    sin = jnp.tile(jnp.sin(ang), (reps, 1)).T
    lane = lax.broadcasted_iota(jnp.int32, (1, LANES), 1) % HEAD_DIM
    c_mul = jnp.where(lane < ROPE_DIM, cos, 1.0)
    s_below = jnp.where((lane >= ROPE_HALF) & (lane < ROPE_DIM), sin, 0.0)
    s_above = jnp.where(lane < ROPE_HALF, -sin, 0.0)

    per_dot = MXU_DIM // LANES
    for c in range(o_ref.shape[0]):
        t2 = _dot(h, w_ref[:, c * MXU_DIM:(c + 1) * MXU_DIM])
        for sub in range(per_dot):
            blk = c * per_dot + sub
            t = t2[:, sub * LANES:(sub + 1) * LANES]
            if blk < n_rope_blocks:
                t = (t * c_mul
                     + pltpu.roll(t, ROPE_HALF, 1) * s_below
                     + pltpu.roll(t, LANES - ROPE_HALF, 1) * s_above)
            if blk in q_blocks:
                t = t * Q_SCALE
            o_ref[c, :, sub * LANES:(sub + 1) * LANES] = t.astype(BF16)


def _in_proj(x2, pos2, g, invf, w_bf, wvt_bf, *, n_rope_blocks, q_blocks):
    n, d = x2.shape
    cols = w_bf.shape[1]
    vrows = wvt_bf.shape[0]
    tm = TOKEN_TILE
    kern = functools.partial(_in_proj_kernel, n_rope_blocks=n_rope_blocks, q_blocks=q_blocks)
    const = lambda i: (0, 0)
    return pl.pallas_call(
        kern,
        out_shape=(jax.ShapeDtypeStruct((cols // MXU_DIM, n, MXU_DIM), BF16), jax.ShapeDtypeStruct((vrows, n), BF16)),
        grid=(n // tm,),
        in_specs=[
            pl.BlockSpec((tm, d), lambda i: (i, 0)),
            pl.BlockSpec((None, 1, tm), lambda i: (i, 0, 0)),
            pl.BlockSpec((1, d), const),
            pl.BlockSpec((ROPE_HALF, 1), const),
            pl.BlockSpec((d, cols), const),
            pl.BlockSpec((vrows, d), const),
        ],
        out_specs=(pl.BlockSpec((cols // MXU_DIM, tm, MXU_DIM), lambda i: (0, i, 0)),
                   pl.BlockSpec((vrows, tm), lambda i: (0, i))),
        compiler_params=pltpu.CompilerParams(vmem_limit_bytes=VMEM_LIMIT_BYTES),
        name="in_proj",
    )(x2, pos2, g, invf, w_bf, wvt_bf)


def _diff_attn_kernel(lam_ref, g_ref, q_ref, k_ref, vt_ref, o_ref, s_a, s_b, m_ref, l_ref, acc_ref,
                      *, lambda_init, heads):
    t = DIFF_TILE
    nq = q_ref.shape[0] // t
    n = 2 * heads

    first = lax.broadcasted_iota(jnp.int32, (1, LANES), 1) < HEAD_DIM
    key = lax.broadcasted_iota(jnp.int32, (t, t), 0)
    qry = lax.broadcasted_iota(jnp.int32, (t, t), 1)

    def block_rows(blk):
        return pl.ds(pl.multiple_of(blk * t, t), t)

    def q_map(c, i):
        q = _lane_block(q_ref, block_rows(i), c // 2)
        zero = jnp.zeros_like(q)
        return jnp.where(first, q, zero) if c % 2 == 0 else jnp.where(first, zero, q)

    def scores_into(s_ref, c, i, blk):
        s_ref[c] = _dot_nt(_lane_block(k_ref, block_rows(blk), c // 2), q_map(c, i))

    def softmax_stage(s_ref, c, blk, causal):
        s = s_ref[c]
        if causal is not None:
            s = jnp.where(causal, s, -jnp.inf)
        m_old = m_ref[c]
        m_new = jnp.maximum(m_old, jnp.max(s, axis=0, keepdims=True))
        alpha = jnp.exp2(m_old - m_new)
        p = jnp.exp2(s - m_new)
        l_ref[c] = alpha * l_ref[c] + jnp.sum(p, axis=0, keepdims=True)
        m_ref[c] = m_new
        h = c // 2
        pv = _dot(vt_ref[h * LANES:(h + 1) * LANES, block_rows(blk)], p.astype(BF16))
        acc_ref[c] = alpha * acc_ref[c] + pv

    def step(s_cur, s_next, i, blk, causal, prefetch=True):
        if prefetch:
            nxt = jnp.maximum(blk - 1, 0)
            for c in range(n):
                scores_into(s_next, c, i, nxt)
        for c in range(n):
            softmax_stage(s_cur, c, blk, causal)

    lam_v = lam_ref[...]
    lam = (jnp.exp(jnp.sum(lam_v[0:1] * lam_v[1:2], axis=-1, keepdims=True))
           - jnp.exp(jnp.sum(lam_v[2:3] * lam_v[3:4], axis=-1, keepdims=True))
           + lambda_init)

    def query_tile(i, _):
        for c in range(n):
            m_ref[c] = jnp.full((1, t), -jnp.inf, F32)
            l_ref[c] = jnp.zeros((1, t), F32)
            acc_ref[c] = jnp.zeros((LANES, t), F32)
        step(s_a, s_b, i, i, key <= qry)

        def pair_body(p, _):
            blk = i - 1 - 2 * p
            step(s_b, s_a, i, blk, None)
            step(s_a, s_b, i, blk - 1, None)
            return 0

        lax.fori_loop(0, jnp.maximum(i - 1, 0) // 2, pair_body, 0)

        @pl.when(i % 2 == 1)
        def _():
            step(s_b, s_a, i, 0, None, prefetch=False)

        @pl.when((i % 2 == 0) & (i >= 2))
        def _():
            step(s_b, s_a, i, 1, None)
            step(s_a, s_b, i, 0, None, prefetch=False)

        nxt_tile = jnp.minimum(i + 1, nq - 1)
        for c in range(n):
            scores_into(s_a, c, nxt_tile, nxt_tile)
        for h in range(heads):
            o_t = acc_ref[2 * h] / l_ref[2 * h] - lam * (acc_ref[2 * h + 1] / l_ref[2 * h + 1])
            o_t = o_t * lax.rsqrt(jnp.mean(o_t * o_t, axis=0, keepdims=True) + NORM_EPS)
            o = o_t.T * g_ref[...] * (1.0 - lambda_init)
            o_ref[block_rows(i), h * LANES:(h + 1) * LANES] = o.astype(o_ref.dtype)
        return 0

    for c in range(n):
        scores_into(s_a, c, 0, 0)
    lax.fori_loop(0, nq, query_tile, 0)


def _diff_attn(proj, v_t, lam_vecs, g, *, batch, seq, heads, q_blk0, k_blk0, vt_blk0, lambda_init):
    t = DIFF_TILE
    hps = DIFF_HEADS_PER_STEP
    assert heads % hps == 0 and q_blk0 % hps == 0 and k_blk0 % hps == 0 and vt_blk0 % hps == 0
    assert proj.shape[2] == hps * LANES
    kern = functools.partial(_diff_attn_kernel, lambda_init=lambda_init, heads=hps)
    w = hps * LANES
    return pl.pallas_call(
        kern,
        out_shape=jax.ShapeDtypeStruct((heads // hps, batch * seq, w), BF16),
        grid=(batch, heads // hps),
        in_specs=[
            pl.BlockSpec((4, HEAD_DIM), lambda b, h: (0, 0)),
            pl.BlockSpec((1, LANES), lambda b, h: (0, 0)),
            pl.BlockSpec((None, seq, w), lambda b, h: (q_blk0 // hps + h, b, 0)),
            pl.BlockSpec((None, seq, w), lambda b, h: (k_blk0 // hps + h, b, 0)),
            pl.BlockSpec((w, seq), lambda b, h: (vt_blk0 // hps + h, b)),
        ],
        out_specs=pl.BlockSpec((None, seq, w), lambda b, h: (h, b, 0)),
        scratch_shapes=[
            pltpu.VMEM((2 * hps, t, t), F32), pltpu.VMEM((2 * hps, t, t), F32),
            pltpu.VMEM((2 * hps, 1, t), F32), pltpu.VMEM((2 * hps, 1, t), F32),
            pltpu.VMEM((2 * hps, LANES, t), F32),
        ],
        compiler_params=pltpu.CompilerParams(vmem_limit_bytes=VMEM_LIMIT_BYTES),
        name="diff_attn",
    )(lam_vecs, g, proj, proj, v_t)


def _sb_attn_kernel(g_ref, q_ref, k_ref, vt_ref, o_ref, ws_a, ws_b, cy_a, cy_b, a_buf, acc_ref, sc_ref, *, pairs):
    t = SB_TILE
    nq = q_ref.shape[0] // t
    n = 2 * pairs

    first = lax.broadcasted_iota(jnp.int32, (1, LANES), 1) < HEAD_DIM
    key = lax.broadcasted_iota(jnp.int32, (t, t), 0)
    qry = lax.broadcasted_iota(jnp.int32, (t, t), 1)
    tri = (qry >= key).astype(BF16)

    def block_rows(blk):
        return pl.ds(pl.multiple_of(blk * t, t), t)

    def q_head(c, i):
        q = _lane_block(q_ref, block_rows(i), c // 2)
        zero = jnp.zeros_like(q)
        return jnp.where(first, q, zero) if c % 2 == 0 else jnp.where(first, zero, q)

    def logits_into(ws, c, i, blk):
        ws[c] = _dot_nt(_lane_block(k_ref, block_rows(blk), c // 2), q_head(c, i))

    def values_into_acc(c, blk):
        v_t = vt_ref[c * HEAD_DIM:(c + 1) * HEAD_DIM, block_rows(blk)]
        acc_ref[c] += sc_ref[c] * _dot(v_t, a_buf[c])

    def cumsum_stage(ws, cy_in, cy_out, c, strict):
        w = ws[c]
        sp = jnp.log(1.0 + jnp.exp2(jnp.minimum(w, SOFTPLUS_CLAMP))) * LOG2E
        neg_l = jnp.maximum(w, sp)
        if strict is not None:
            neg_l = jnp.where(strict, neg_l, 0.0)
        incl = _dot(tri, neg_l.astype(BF16))
        cy_out[c] = cy_in[c] + incl[0:1, :]
        return incl

    def weight_stage(ws, cy_in, c, incl, strict):
        a = jnp.exp2(ws[c] - incl)
        if strict is not None:
            a = jnp.where(strict, a, 0.0)
        a_buf[c] = a.astype(BF16)
        sc_ref[c] = jnp.exp2(-cy_in[c])

    def step(ws, ws_next, cy_in, cy_out, i, blk, strict, prev_values, prefetch=True):
        nxt = jnp.maximum(blk - 1, 0)
        incl = [None] * n
        for c in range(n):
            if prefetch and c == 0:
                logits_into(ws_next, 0, i, nxt)
            if prefetch and c + 1 < n:
                logits_into(ws_next, c + 1, i, nxt)
            incl[c] = cumsum_stage(ws, cy_in, cy_out, c, strict)
            if prev_values:
                values_into_acc(c, blk + 1)
        for c in range(n):
            weight_stage(ws, cy_in, c, incl[c], strict)

    def query_tile(i, _):
        for c in range(n):
            cy_a[c] = jnp.zeros((1, t), F32)
            acc_ref[c] = jnp.zeros((HEAD_DIM, t), F32)
        step(ws_a, ws_b, cy_a, cy_b, i, i, key < qry, False)

        def pair_body(p, _):
            blk = i - 1 - 2 * p
            step(ws_b, ws_a, cy_b, cy_a, i, blk, None, True)
            step(ws_a, ws_b, cy_a, cy_b, i, blk - 1, None, True)
            return 0

        lax.fori_loop(0, jnp.maximum(i - 1, 0) // 2, pair_body, 0)

        @pl.when(i % 2 == 1)
        def _():
            step(ws_b, ws_a, cy_b, cy_a, i, 0, None, True, prefetch=False)

        @pl.when((i % 2 == 0) & (i >= 2))
        def _():
            step(ws_b, ws_a, cy_b, cy_a, i, 1, None, True)
            step(ws_a, ws_b, cy_a, cy_b, i, 0, None, True, prefetch=False)

        for c in range(n):
            values_into_acc(c, 0)
        nxt_tile = jnp.minimum(i + 1, nq - 1)
        for c in range(n):
            logits_into(ws_a, c, nxt_tile, nxt_tile)
        for p in range(pairs):
            halves = []
            for c in (2 * p, 2 * p + 1):
                o_t = acc_ref[c]
                halves.append(o_t * lax.rsqrt(jnp.mean(o_t * o_t, axis=0, keepdims=True) + NORM_EPS))
            o = jnp.concatenate(halves, axis=0).T * g_ref[...]
            o_ref[block_rows(i), p * LANES:(p + 1) * LANES] = o.astype(o_ref.dtype)
        return 0

    for c in range(n):
        logits_into(ws_a, c, 0, 0)
    lax.fori_loop(0, nq, query_tile, 0)


def _sb_attn(proj, v_t, g2, *, batch, seq, pairs, q_blk0, k_blk0, vt_blk0):
    t = SB_TILE
    pps = SB_PAIRS_PER_STEP
    assert pairs % pps == 0 and q_blk0 % pps == 0 and k_blk0 % pps == 0 and vt_blk0 % pps == 0
    assert proj.shape[2] == pps * LANES
    w = pps * LANES
    return pl.pallas_call(
        functools.partial(_sb_attn_kernel, pairs=pps),
        out_shape=jax.ShapeDtypeStruct((pairs // pps, batch * seq, w), BF16),
        grid=(batch, pairs // pps),
        in_specs=[
            pl.BlockSpec((1, LANES), lambda b, h: (0, 0)),
            pl.BlockSpec((None, seq, w), lambda b, h: (q_blk0 // pps + h, b, 0)),
            pl.BlockSpec((None, seq, w), lambda b, h: (k_blk0 // pps + h, b, 0)),
            pl.BlockSpec((w, seq), lambda b, h: (vt_blk0 // pps + h, b)),
        ],
        out_specs=pl.BlockSpec((None, seq, w), lambda b, h: (h, b, 0)),
        scratch_shapes=[
            pltpu.VMEM((2 * pps, t, t), F32), pltpu.VMEM((2 * pps, t, t), F32),
            pltpu.VMEM((2 * pps, 1, t), F32), pltpu.VMEM((2 * pps, 1, t), F32),
            pltpu.VMEM((2 * pps, t, t), BF16),
            pltpu.VMEM((2 * pps, HEAD_DIM, t), F32),
            pltpu.VMEM((2 * pps, 1, t), F32),
        ],
        compiler_params=pltpu.CompilerParams(vmem_limit_bytes=VMEM_LIMIT_BYTES),
        name="sb_attn",
    )(g2, proj, proj, v_t)


def _out_ffn_kernel(x_ref, d_ref, s_ref, wo_ref, g2_ref, wg_ref, wu_ref, wd_ref, gf_ref,
                    o_ref, act_ref):
    x1 = x_ref[...]
    row = 0
    for mix_ref in (d_ref, s_ref):
        for grp in range(mix_ref.shape[0]):
            gw = mix_ref.shape[2]
            x1 = x1 + _dot(mix_ref[grp], wo_ref[row:row + gw, :])
            row += gw
    h2 = (x1 * _rms_scale(x1) * g2_ref[...]).astype(BF16)
    d_ff = wg_ref.shape[1]
    for c in range(d_ff // FFN_CHUNK):
        cols = slice(c * FFN_CHUNK, (c + 1) * FFN_CHUNK)
        gate = _dot(h2, wg_ref[:, cols])
        up = _dot(h2, wu_ref[:, cols])
        act_ref[:, cols] = (gate * jax.nn.sigmoid(gate) * up).astype(BF16)
    y = x1 + _dot(act_ref[...], wd_ref[...])
    o_ref[...] = y * _rms_scale(y) * gf_ref[...]


def _out_ffn(x2, d_out, s_out, wo, g2, wg, wu, wd, gf):
    n, d = x2.shape
    d_ff = wg.shape[1]
    tm = TOKEN_TILE
    assert d_ff % FFN_CHUNK == 0
    const = lambda i: (0, 0)
    resident = functools.partial(pl.BlockSpec, index_map=const, pipeline_mode=pl.Buffered(1))
    return pl.pallas_call(
        _out_ffn_kernel,
        out_shape=jax.ShapeDtypeStruct((n, d), F32),
        grid=(n // tm,),
        in_specs=[
            pl.BlockSpec((tm, d), lambda i: (i, 0)),
            pl.BlockSpec((d_out.shape[0], tm, d_out.shape[2]), lambda i: (0, i, 0)),
            pl.BlockSpec((s_out.shape[0], tm, s_out.shape[2]), lambda i: (0, i, 0)),
            resident(wo.shape),
            pl.BlockSpec((1, d), const),
            resident(wg.shape),
            resident(wu.shape),
            resident(wd.shape),
            pl.BlockSpec((1, d), const),
        ],
        out_specs=pl.BlockSpec((tm, d), lambda i: (i, 0)),
        scratch_shapes=[pltpu.VMEM((tm, d_ff), BF16)],
        compiler_params=pltpu.CompilerParams(vmem_limit_bytes=VMEM_LIMIT_BYTES),
        name="out_ffn",
    )(x2, d_out, s_out, wo, g2, wg, wu, wd, gf)


def kernel(x, positions, norm_attn_g, w_in, lambda_q1, lambda_k1, lambda_q2, lambda_k2,
           diff_subln_g, sb_norm_g, w_out, norm_ffn_g, w_gate, w_up, w_down, final_norm_g):
    b, s, d = x.shape
    depth = w_in.shape[0]
    assert depth == 1
    diff_heads = d // (2 * 2 * HEAD_DIM)
    sb_pairs = d // (2 * HEAD_DIM) // 2
    assert diff_subln_g.shape[1] == LANES and sb_norm_g.shape[1] == HEAD_DIM
    assert w_in.shape[2] == 3 * diff_heads * LANES + 3 * sb_pairs * LANES
    dq0, dk0 = 0, diff_heads
    sq0 = 2 * diff_heads
    sk0 = sq0 + sb_pairs
    q_blocks = frozenset(range(dq0, dk0)) | frozenset(range(sq0, sk0))
    dvt0, svt0 = 0, diff_heads

    x2 = x.reshape(b * s, d)
    pos2 = positions.astype(F32).reshape(b * s // TOKEN_TILE, 1, TOKEN_TILE)
    inv_freq = ROPE_THETA ** (-jnp.arange(ROPE_HALF, dtype=F32) / ROPE_HALF)
    invf = inv_freq.reshape(ROPE_HALF, 1)

    layer = 0
    lambda_init = 0.8 - 0.6 * math.exp(-0.3 * layer)
    qk_w = 2 * diff_heads * LANES
    v_w = diff_heads * LANES
    sb_w = sb_pairs * LANES
    w = w_in[layer].astype(BF16)
    w_rows = jnp.concatenate([w[:, :qk_w], w[:, qk_w + v_w:qk_w + v_w + 2 * sb_w]], axis=1)
    w_vt = jnp.concatenate([w[:, qk_w:qk_w + v_w], w[:, qk_w + v_w + 2 * sb_w:]], axis=1).T
    proj, v_t = _in_proj(x2, pos2, norm_attn_g[layer].reshape(1, d), invf, w_rows, w_vt,
                         n_rope_blocks=sq0, q_blocks=q_blocks)
    lam_vecs = jnp.stack([lambda_q1[layer], lambda_k1[layer], lambda_q2[layer], lambda_k2[layer]])
    d_out = _diff_attn(proj, v_t, lam_vecs.astype(F32), diff_subln_g[layer].reshape(1, LANES),
                       batch=b, seq=s, heads=diff_heads, q_blk0=dq0, k_blk0=dk0, vt_blk0=dvt0,
                       lambda_init=lambda_init)
    sb_g2 = jnp.tile(sb_norm_g[layer], 2).reshape(1, LANES)
    s_out = _sb_attn(proj, v_t, sb_g2, batch=b, seq=s, pairs=sb_pairs,
                     q_blk0=sq0, k_blk0=sk0, vt_blk0=svt0)
    out = _out_ffn(x2, d_out, s_out, w_out[layer].astype(BF16), norm_ffn_g[layer].reshape(1, d),
                   w_gate[layer].astype(BF16), w_up[layer].astype(BF16),
                   w_down[layer].astype(BF16), final_norm_g.reshape(1, d))
    return out.reshape(b, s, d)
```

```python
import functools
import math

import jax
import jax.numpy as jnp
from jax import lax
from jax.experimental import pallas as pl
from jax.experimental.pallas import tpu as pltpu

F32 = jnp.float32
BF16 = jnp.bfloat16

LANES = 128
MXU_DIM = 256
HEAD_DIM = 64
ROPE_DIM = HEAD_DIM // 4
ROPE_HALF = ROPE_DIM // 2
ROPE_THETA = 500000.0
NORM_EPS = 1e-5
LOG2E = math.log2(math.e)
Q_SCALE = HEAD_DIM ** -0.5 * LOG2E
SOFTPLUS_CLAMP = 64.0

VMEM_LIMIT_BYTES = 56 * 1024 * 1024

TOKEN_TILE = 512
IN_PROJ_TILE = 1024
DIFF_TILE = 512
SB_TILE = 256
DIFF_HEADS_PER_STEP = 2
SB_PAIRS_PER_STEP = 2
FFN_CHUNK = 256


def _rms_scale(x, eps=NORM_EPS):
    return lax.rsqrt(jnp.mean(x * x, axis=-1, keepdims=True) + eps)


def _dot(a, b):
    return jnp.dot(a, b, preferred_element_type=F32)


def _dot_nt(a, b):
    return lax.dot_general(a, b, (((1,), (1,)), ((), ())), preferred_element_type=F32)


def _lane_block(ref, rows, blk):
    return ref[rows, blk * LANES:(blk + 1) * LANES]


def _in_proj_kernel(x_ref, pos_ref, g_ref, invf_ref, w_ref, wvt_ref, o_ref, vt_ref, *, n_rope_blocks, q_blocks):
    x = x_ref[...]
    h = (x * _rms_scale(x) * g_ref[...]).astype(BF16)

    vt_ref[...] = _dot_nt(wvt_ref[...], h).astype(BF16)

    ang = invf_ref[...] * pos_ref[...]
    reps = LANES // ROPE_HALF
    cos = jnp.tile(jnp.cos(ang), (reps, 1)).T
    sin = jnp.tile(jnp.sin(ang), (reps, 1)).T
    lane = lax.broadcasted_iota(jnp.int32, (1, LANES), 1) % HEAD_DIM
    c_mul = jnp.where(lane < ROPE_DIM, cos, 1.0)
    s_below = jnp.where((lane >= ROPE_HALF) & (lane < ROPE_DIM), sin, 0.0)
    s_above = jnp.where(lane < ROPE_HALF, -sin, 0.0)

    per_dot = MXU_DIM // LANES
    for c in range(o_ref.shape[0]):
        t2 = _dot(h, w_ref[:, c * MXU_DIM:(c + 1) * MXU_DIM])
        for sub in range(per_dot):
            blk = c * per_dot + sub
            t = t2[:, sub * LANES:(sub + 1) * LANES]
            if blk < n_rope_blocks:
                t = (t * c_mul
                     + pltpu.roll(t, ROPE_HALF, 1) * s_below
                     + pltpu.roll(t, LANES - ROPE_HALF, 1) * s_above)
            if blk in q_blocks:
                t = t * Q_SCALE
            o_ref[c, :, sub * LANES:(sub + 1) * LANES] = t.astype(BF16)


def _in_proj(x2, pos2, g, invf, w_bf, wvt_bf, *, n_rope_blocks, q_blocks):
    n, d = x2.shape
    cols = w_bf.shape[1]
    vrows = wvt_bf.shape[0]
    tm = IN_PROJ_TILE
    kern = functools.partial(_in_proj_kernel, n_rope_blocks=n_rope_blocks, q_blocks=q_blocks)
    const = lambda i: (0, 0)
    return pl.pallas_call(
        kern,
        out_shape=(jax.ShapeDtypeStruct((cols // MXU_DIM, n, MXU_DIM), BF16), jax.ShapeDtypeStruct((vrows, n), BF16)),
        grid=(n // tm,),
        in_specs=[
            pl.BlockSpec((tm, d), lambda i: (i, 0)),
            pl.BlockSpec((None, 1, tm), lambda i: (i, 0, 0)),
            pl.BlockSpec((1, d), const),
            pl.BlockSpec((ROPE_HALF, 1), const),
            pl.BlockSpec((d, cols), const),
            pl.BlockSpec((vrows, d), const),
        ],
        out_specs=(pl.BlockSpec((cols // MXU_DIM, tm, MXU_DIM), lambda i: (0, i, 0)),
                   pl.BlockSpec((vrows, tm), lambda i: (0, i))),
        compiler_params=pltpu.CompilerParams(vmem_limit_bytes=VMEM_LIMIT_BYTES),
        name="in_proj",
    )(x2, pos2, g, invf, w_bf, wvt_bf)


def _diff_attn_kernel(lam_ref, g_ref, q_ref, k_ref, vt_ref, o_ref, s_a, s_b, m_ref, l_ref, acc_ref,
                      *, lambda_init, heads):
    t = DIFF_TILE
    nq = q_ref.shape[0] // t
    n = 2 * heads

    first = lax.broadcasted_iota(jnp.int32, (1, LANES), 1) < HEAD_DIM
    key = lax.broadcasted_iota(jnp.int32, (t, t), 0)
    qry = lax.broadcasted_iota(jnp.int32, (t, t), 1)

    def block_rows(blk):
        return pl.ds(pl.multiple_of(blk * t, t), t)

    def q_map(c, i):
        q = _lane_block(q_ref, block_rows(i), c // 2)
        zero = jnp.zeros_like(q)
        return jnp.where(first, q, zero) if c % 2 == 0 else jnp.where(first, zero, q)

    def scores_into(s_ref, c, i, blk):
        s_ref[c] = _dot_nt(_lane_block(k_ref, block_rows(blk), c // 2), q_map(c, i))

    def softmax_stage(s_ref, c, blk, causal):
        s = s_ref[c]
        if causal is not None:
            s = jnp.where(causal, s, -jnp.inf)
        m_old = m_ref[c]
        m_new = jnp.maximum(m_old, jnp.max(s, axis=0, keepdims=True))
        alpha = jnp.exp2(m_old - m_new)
        p = jnp.exp2(s - m_new)
        l_ref[c] = alpha * l_ref[c] + jnp.sum(p, axis=0, keepdims=True)
        m_ref[c] = m_new
        h = c // 2
        pv = _dot(vt_ref[h * LANES:(h + 1) * LANES, block_rows(blk)], p.astype(BF16))
        acc_ref[c] = alpha * acc_ref[c] + pv

    def step(s_cur, s_next, i, blk, causal, prefetch=True):
        if prefetch:
            nxt = jnp.maximum(blk - 1, 0)
            for c in range(n):
                scores_into(s_next, c, i, nxt)
        for c in range(n):
            softmax_stage(s_cur, c, blk, causal)

    lam_v = lam_ref[...]
    lam = (jnp.exp(jnp.sum(lam_v[0:1] * lam_v[1:2], axis=-1, keepdims=True))
           - jnp.exp(jnp.sum(lam_v[2:3] * lam_v[3:4], axis=-1, keepdims=True))
           + lambda_init)

    def query_tile(i, _):
        for c in range(n):
            m_ref[c] = jnp.full((1, t), -jnp.inf, F32)
            l_ref[c] = jnp.zeros((1, t), F32)
            acc_ref[c] = jnp.zeros((LANES, t), F32)
        step(s_a, s_b, i, i, key <= qry)

        def pair_body(p, _):
            blk = i - 1 - 2 * p
            step(s_b, s_a, i, blk, None)
            step(s_a, s_b, i, blk - 1, None)
            return 0

        lax.fori_loop(0, jnp.maximum(i - 1, 0) // 2, pair_body, 0)

        @pl.when(i % 2 == 1)
        def _():
            step(s_b, s_a, i, 0, None, prefetch=False)

        @pl.when((i % 2 == 0) & (i >= 2))
        def _():
            step(s_b, s_a, i, 1, None)
            step(s_a, s_b, i, 0, None, prefetch=False)

        nxt_tile = jnp.minimum(i + 1, nq - 1)
        for c in range(n):
            scores_into(s_a, c, nxt_tile, nxt_tile)
        for h in range(heads):
            o_t = acc_ref[2 * h] / l_ref[2 * h] - lam * (acc_ref[2 * h + 1] / l_ref[2 * h + 1])
            o_t = o_t * lax.rsqrt(jnp.mean(o_t * o_t, axis=0, keepdims=True) + NORM_EPS)
            o = o_t.T * g_ref[...] * (1.0 - lambda_init)
            o_ref[block_rows(i), h * LANES:(h + 1) * LANES] = o.astype(o_ref.dtype)
        return 0

    for c in range(n):
        scores_into(s_a, c, 0, 0)
    lax.fori_loop(0, nq, query_tile, 0)


def _diff_attn(proj, v_t, lam_vecs, g, *, batch, seq, heads, q_blk0, k_blk0, vt_blk0, lambda_init):
    t = DIFF_TILE
    hps = DIFF_HEADS_PER_STEP
    assert heads % hps == 0 and q_blk0 % hps == 0 and k_blk0 % hps == 0 and vt_blk0 % hps == 0
    assert proj.shape[2] == hps * LANES
    kern = functools.partial(_diff_attn_kernel, lambda_init=lambda_init, heads=hps)
    w = hps * LANES
    return pl.pallas_call(
        kern,
        out_shape=jax.ShapeDtypeStruct((heads // hps, batch * seq, w), BF16),
        grid=(batch, heads // hps),
        in_specs=[
            pl.BlockSpec((4, HEAD_DIM), lambda b, h: (0, 0)),
            pl.BlockSpec((1, LANES), lambda b, h: (0, 0)),
            pl.BlockSpec((None, seq, w), lambda b, h: (q_blk0 // hps + h, b, 0)),
            pl.BlockSpec((None, seq, w), lambda b, h: (k_blk0 // hps + h, b, 0)),
            pl.BlockSpec((w, seq), lambda b, h: (vt_blk0 // hps + h, b)),
        ],
        out_specs=pl.BlockSpec((None, seq, w), lambda b, h: (h, b, 0)),
        scratch_shapes=[
            pltpu.VMEM((2 * hps, t, t), F32), pltpu.VMEM((2 * hps, t, t), F32),
            pltpu.VMEM((2 * hps, 1, t), F32), pltpu.VMEM((2 * hps, 1, t), F32),
            pltpu.VMEM((2 * hps, LANES, t), F32),
        ],
        compiler_params=pltpu.CompilerParams(vmem_limit_bytes=VMEM_LIMIT_BYTES),
        name="diff_attn",
    )(lam_vecs, g, proj, proj, v_t)


def _sb_attn_kernel(g_ref, q_ref, k_ref, vt_ref, o_ref, ws_a, ws_b, cy_a, cy_b, a_buf, acc_ref, sc_ref, *, pairs):
    t = SB_TILE
    nq = q_ref.shape[0] // t
    n = 2 * pairs

    first = lax.broadcasted_iota(jnp.int32, (1, LANES), 1) < HEAD_DIM
    key = lax.broadcasted_iota(jnp.int32, (t, t), 0)
    qry = lax.broadcasted_iota(jnp.int32, (t, t), 1)
    tri = (qry >= key).astype(BF16)

    def block_rows(blk):
        return pl.ds(pl.multiple_of(blk * t, t), t)

    def q_head(c, i):
        q = _lane_block(q_ref, block_rows(i), c // 2)
        zero = jnp.zeros_like(q)
        return jnp.where(first, q, zero) if c % 2 == 0 else jnp.where(first, zero, q)

    def logits_into(ws, c, i, blk):
        ws[c] = _dot_nt(_lane_block(k_ref, block_rows(blk), c // 2), q_head(c, i))

    def values_into_acc(c, blk):
        v_t = vt_ref[c * HEAD_DIM:(c + 1) * HEAD_DIM, block_rows(blk)]
        acc_ref[c] += sc_ref[c] * _dot(v_t, a_buf[c])

    def cumsum_stage(ws, cy_in, cy_out, c, strict):
        w = ws[c]
        sp = jnp.log(1.0 + jnp.exp2(jnp.minimum(w, SOFTPLUS_CLAMP))) * LOG2E
        neg_l = jnp.maximum(w, sp)
        if strict is not None:
            neg_l = jnp.where(strict, neg_l, 0.0)
        incl = _dot(tri, neg_l.astype(BF16))
        cy_out[c] = cy_in[c] + incl[0:1, :]
        return incl

    def weight_stage(ws, cy_in, c, incl, strict):
        a = jnp.exp2(ws[c] - incl)
        if strict is not None:
            a = jnp.where(strict, a, 0.0)
        a_buf[c] = a.astype(BF16)
        sc_ref[c] = jnp.exp2(-cy_in[c])

    def step(ws, ws_next, cy_in, cy_out, i, blk, strict, prev_values, prefetch=True):
        nxt = jnp.maximum(blk - 1, 0)
        incl = [None] * n
        for c in range(n):
            if prefetch and c == 0:
                logits_into(ws_next, 0, i, nxt)
            if prefetch and c + 1 < n:
                logits_into(ws_next, c + 1, i, nxt)
            incl[c] = cumsum_stage(ws, cy_in, cy_out, c, strict)
            if prev_values:
                values_into_acc(c, blk + 1)
        for c in range(n):
            weight_stage(ws, cy_in, c, incl[c], strict)

    def query_tile(i, _):
        for c in range(n):
            cy_a[c] = jnp.zeros((1, t), F32)
            acc_ref[c] = jnp.zeros((HEAD_DIM, t), F32)
        step(ws_a, ws_b, cy_a, cy_b, i, i, key < qry, False)

        def pair_body(p, _):
            blk = i - 1 - 2 * p
            step(ws_b, ws_a, cy_b, cy_a, i, blk, None, True)
            step(ws_a, ws_b, cy_a, cy_b, i, blk - 1, None, True)
            return 0

        lax.fori_loop(0, jnp.maximum(i - 1, 0) // 2, pair_body, 0)

        @pl.when(i % 2 == 1)
        def _():
            step(ws_b, ws_a, cy_b, cy_a, i, 0, None, True, prefetch=False)

        @pl.when((i % 2 == 0) & (i >= 2))
        def _():
            step(ws_b, ws_a, cy_b, cy_a, i, 1, None, True)
            step(ws_a, ws_b, cy_a, cy_b, i, 0, None, True, prefetch=False)

        for c in range(n):
            values_into_acc(c, 0)
        nxt_tile = jnp.minimum(i + 1, nq - 1)
        for c in range(n):
            logits_into(ws_a, c, nxt_tile, nxt_tile)
        for p in range(pairs):
            halves = []
            for c in (2 * p, 2 * p + 1):
                o_t = acc_ref[c]
                halves.append(o_t * lax.rsqrt(jnp.mean(o_t * o_t, axis=0, keepdims=True) + NORM_EPS))
            o = jnp.concatenate(halves, axis=0).T * g_ref[...]
            o_ref[block_rows(i), p * LANES:(p + 1) * LANES] = o.astype(o_ref.dtype)
        return 0

    for c in range(n):
        logits_into(ws_a, c, 0, 0)
    lax.fori_loop(0, nq, query_tile, 0)


def _sb_attn(proj, v_t, g2, *, batch, seq, pairs, q_blk0, k_blk0, vt_blk0):
    t = SB_TILE
    pps = SB_PAIRS_PER_STEP
    assert pairs % pps == 0 and q_blk0 % pps == 0 and k_blk0 % pps == 0 and vt_blk0 % pps == 0
    assert proj.shape[2] == pps * LANES
    w = pps * LANES
    return pl.pallas_call(
        functools.partial(_sb_attn_kernel, pairs=pps),
        out_shape=jax.ShapeDtypeStruct((pairs // pps, batch * seq, w), BF16),
        grid=(batch, pairs // pps),
        in_specs=[
            pl.BlockSpec((1, LANES), lambda b, h: (0, 0)),
            pl.BlockSpec((None, seq, w), lambda b, h: (q_blk0 // pps + h, b, 0)),
            pl.BlockSpec((None, seq, w), lambda b, h: (k_blk0 // pps + h, b, 0)),
            pl.BlockSpec((w, seq), lambda b, h: (vt_blk0 // pps + h, b)),
        ],
        out_specs=pl.BlockSpec((None, seq, w), lambda b, h: (h, b, 0)),
        scratch_shapes=[
            pltpu.VMEM((2 * pps, t, t), F32), pltpu.VMEM((2 * pps, t, t), F32),
            pltpu.VMEM((2 * pps, 1, t), F32), pltpu.VMEM((2 * pps, 1, t), F32),
            pltpu.VMEM((2 * pps, t, t), BF16),
            pltpu.VMEM((2 * pps, HEAD_DIM, t), F32),
            pltpu.VMEM((2 * pps, 1, t), F32),
        ],
        compiler_params=pltpu.CompilerParams(vmem_limit_bytes=VMEM_LIMIT_BYTES),
        name="sb_attn",
    )(g2, proj, proj, v_t)


def _out_ffn_kernel(x_ref, d_ref, s_ref, wo_ref, g2_ref, wg_ref, wu_ref, wd_ref, gf_ref,
                    o_ref, act_ref):
    x1 = x_ref[...]
    row = 0
    for mix_ref in (d_ref, s_ref):
        for grp in range(mix_ref.shape[0]):
            gw = mix_ref.shape[2]
            x1 = x1 + _dot(mix_ref[grp], wo_ref[row:row + gw, :])
            row += gw
    h2 = (x1 * _rms_scale(x1) * g2_ref[...]).astype(BF16)
    d_ff = wg_ref.shape[1]
    for c in range(d_ff // FFN_CHUNK):
        cols = slice(c * FFN_CHUNK, (c + 1) * FFN_CHUNK)
        gate = _dot(h2, wg_ref[:, cols])
        up = _dot(h2, wu_ref[:, cols])
        act_ref[:, cols] = (gate * jax.nn.sigmoid(gate) * up).astype(BF16)
    y = x1 + _dot(act_ref[...], wd_ref[...])
    o_ref[...] = y * _rms_scale(y) * gf_ref[...]


def _out_ffn(x2, d_out, s_out, wo, g2, wg, wu, wd, gf):
    n, d = x2.shape
    d_ff = wg.shape[1]
    tm = TOKEN_TILE
    assert d_ff % FFN_CHUNK == 0
    const = lambda i: (0, 0)
    resident = functools.partial(pl.BlockSpec, index_map=const, pipeline_mode=pl.Buffered(1))
    return pl.pallas_call(
        _out_ffn_kernel,
        out_shape=jax.ShapeDtypeStruct((n, d), F32),
        grid=(n // tm,),
        in_specs=[
            pl.BlockSpec((tm, d), lambda i: (i, 0)),
            pl.BlockSpec((d_out.shape[0], tm, d_out.shape[2]), lambda i: (0, i, 0)),
            pl.BlockSpec((s_out.shape[0], tm, s_out.shape[2]), lambda i: (0, i, 0)),
            resident(wo.shape),
            pl.BlockSpec((1, d), const),
            resident(wg.shape),
            resident(wu.shape),
            resident(wd.shape),
            pl.BlockSpec((1, d), const),
        ],
        out_specs=pl.BlockSpec((tm, d), lambda i: (i, 0)),
        scratch_shapes=[pltpu.VMEM((tm, d_ff), BF16)],
        compiler_params=pltpu.CompilerParams(vmem_limit_bytes=VMEM_LIMIT_BYTES),
        name="out_ffn",
    )(x2, d_out, s_out, wo, g2, wg, wu, wd, gf)


def kernel(x, positions, norm_attn_g, w_in, lambda_q1, lambda_k1, lambda_q2, lambda_k2,
           diff_subln_g, sb_norm_g, w_out, norm_ffn_g, w_gate, w_up, w_down, final_norm_g):
    b, s, d = x.shape
    depth = w_in.shape[0]
    assert depth == 1
    diff_heads = d // (2 * 2 * HEAD_DIM)
    sb_pairs = d // (2 * HEAD_DIM) // 2
    assert diff_subln_g.shape[1] == LANES and sb_norm_g.shape[1] == HEAD_DIM
    assert w_in.shape[2] == 3 * diff_heads * LANES + 3 * sb_pairs * LANES
    dq0, dk0 = 0, diff_heads
    sq0 = 2 * diff_heads
    sk0 = sq0 + sb_pairs
    q_blocks = frozenset(range(dq0, dk0)) | frozenset(range(sq0, sk0))
    dvt0, svt0 = 0, diff_heads

    x2 = x.reshape(b * s, d)
    pos2 = positions.astype(F32).reshape(b * s // IN_PROJ_TILE, 1, IN_PROJ_TILE)
    inv_freq = ROPE_THETA ** (-jnp.arange(ROPE_HALF, dtype=F32) / ROPE_HALF)
    invf = inv_freq.reshape(ROPE_HALF, 1)

    layer = 0
    lambda_init = 0.8 - 0.6 * math.exp(-0.3 * layer)
    qk_w = 2 * diff_heads * LANES
    v_w = diff_heads * LANES
    sb_w = sb_pairs * LANES
    w = w_in[layer].astype(BF16)
    w_rows = jnp.concatenate([w[:, :qk_w], w[:, qk_w + v_w:qk_w + v_w + 2 * sb_w]], axis=1)
    w_vt = jnp.concatenate([w[:, qk_w:qk_w + v_w], w[:, qk_w + v_w + 2 * sb_w:]], axis=1).T
    proj, v_t = _in_proj(x2, pos2, norm_attn_g[layer].reshape(1, d), invf, w_rows, w_vt,
                         n_rope_blocks=sq0, q_blocks=q_blocks)
    lam_vecs = jnp.stack([lambda_q1[layer], lambda_k1[layer], lambda_q2[layer], lambda_k2[layer]])
    d_out = _diff_attn(proj, v_t, lam_vecs.astype(F32), diff_subln_g[layer].reshape(1, LANES),
                       batch=b, seq=s, heads=diff_heads, q_blk0=dq0, k_blk0=dk0, vt_blk0=dvt0,
                       lambda_init=lambda_init)
    sb_g2 = jnp.tile(sb_norm_g[layer], 2).reshape(1, LANES)
    s_out = _sb_attn(proj, v_t, sb_g2, batch=b, seq=s, pairs=sb_pairs,
                     q_blk0=sq0, k_blk0=sk0, vt_blk0=svt0)
    out = _out_ffn(x2, d_out, s_out, w_out[layer].astype(BF16), norm_ffn_g[layer].reshape(1, d),
                   w_gate[layer].astype(BF16), w_up[layer].astype(BF16),
                   w_down[layer].astype(BF16), final_norm_g.reshape(1, d))
    return out.reshape(b, s, d)
```
